```python
import jax, jax.numpy as jnp
from jax import lax
import numpy as np

D_MODEL = 2048
BATCH = 2
SEQ = 8192
DEPTH = 1

D_FF = 5632
POOL_WINDOWS = (2, 4, 8, 16)
POOL_GROUPS = 4
POOL_GROUP_DIM = 256
POOL_WIDTH = POOL_GROUPS * POOL_GROUP_DIM
N_HEADS = 16
N_KV_GROUPS = 4
HEADS_PER_GROUP = N_HEADS // N_KV_GROUPS
HEAD_DIM = 128
Q_WIDTH = N_HEADS * HEAD_DIM
KV_WIDTH = N_KV_GROUPS * HEAD_DIM
CMP_BLOCK = 32
CMP_STRIDE = 16
CMP_HIDDEN = 256
SEL_BLOCK = 64
SEL_TOPK = 16
WINDOW = 512
NSA_Q_BLOCK = 64
N_NSA_BRANCHES = 3
FORCE_BONUS = 1000.0
IN_WIDTH = POOL_WIDTH + Q_WIDTH + 6 * KV_WIDTH + N_NSA_BRANCHES * N_HEADS + 2 * D_MODEL
RMS_EPS = 1e-6

kernel_name = 'hybrid_pool_nsa_macaron_block'


def rms_norm(x, g):
    xf = x.astype(jnp.float32)
    y = xf * lax.rsqrt(jnp.mean(xf * xf, axis=-1, keepdims=True) + RMS_EPS)
    return (y * g.astype(jnp.float32)).astype(x.dtype)


def swiglu(x, w_gate, w_up, w_down):
    return (jax.nn.silu(x @ w_gate) * (x @ w_up)) @ w_down


def masked_softmax(s, mask):
    s = jnp.where(mask, s.astype(jnp.float32), -jnp.inf)
    m = jnp.max(s, axis=-1, keepdims=True)
    m = jnp.where(jnp.isfinite(m), m, 0.0)
    p = jnp.exp(s - m)
    return p / jnp.maximum(jnp.sum(p, axis=-1, keepdims=True), 1e-30)


def pool_mixer(xp, pool_w, pool_scale):
    B, S, _ = xp.shape
    xf = xp.astype(jnp.float32)
    csum = jnp.concatenate([jnp.zeros((B, 1, POOL_WIDTH), jnp.float32), jnp.cumsum(xf, axis=1)], axis=1)
    t = jnp.arange(S)
    outs = []
    for gi, win in enumerate(POOL_WINDOWS):
        sl = slice(gi * POOL_GROUP_DIM, (gi + 1) * POOL_GROUP_DIM)
        cg = csum[..., sl]
        lo = jnp.maximum(t + 1 - win, 0)
        cnt = jnp.minimum(t + 1, win).astype(jnp.float32)
        outs.append((cg[:, 1:] - cg[:, lo]) / cnt[None, :, None] - xf[..., sl])
    pooled = jnp.stack(outs, axis=2).astype(xp.dtype)
    y = jnp.einsum('bsgc,gcd->bsgd', pooled, pool_w).reshape(B, S, POOL_WIDTH)
    return y * pool_scale


def compress_blocks(t, pe, w1, w2):
    B, S, G, Dh = t.shape
    n_sub = CMP_BLOCK // CMP_STRIDE
    n_chunk = S // CMP_STRIDE
    n_cmp = n_chunk - n_sub + 1
    c = t.reshape(B, n_chunk, CMP_STRIDE, G, Dh)
    blocks = jnp.concatenate([c[:, i:i + n_cmp] for i in range(n_sub)], axis=2)
    blocks = blocks + pe[None, None, :, None, :]
    flat = jnp.transpose(blocks, (0, 1, 3, 2, 4)).reshape(B, n_cmp, G, CMP_BLOCK * Dh)
    return jax.nn.silu(flat @ w1) @ w2


def nsa_mixer(q, kc, vc, ks, vs, kw, vw, gate_logits, pe_k, pe_v, k_w1, k_w2, v_w1, v_w2):
    B, S = q.shape[0], q.shape[1]
    G, HPG, DH = N_KV_GROUPS, HEADS_PER_GROUP, HEAD_DIM
    q = q.reshape(B, S, G, HPG, DH) * (DH ** -0.5)
    kc, vc, ks, vs, kw, vw = [a.reshape(B, S, G, DH) for a in (kc, vc, ks, vs, kw, vw)]
    gates = jax.nn.sigmoid(gate_logits.reshape(B, S, N_NSA_BRANCHES, G, HPG))

    k_cmp = compress_blocks(kc, pe_k, k_w1, k_w2)
    v_cmp = compress_blocks(vc, pe_v, v_w1, v_w2)
    n_cmp = k_cmp.shape[1]
    cmp_start = jnp.arange(n_cmp) * CMP_STRIDE
    cmp_end = cmp_start + CMP_BLOCK - 1

    n_blk = S // SEL_BLOCK
    n_sel = min(SEL_TOPK, n_blk)
    blk = jnp.arange(n_blk)
    sel_start = blk * SEL_BLOCK
    overlap = ((cmp_start[:, None] <= sel_start[None, :] + SEL_BLOCK - 1)
               & (cmp_end[:, None] >= sel_start[None, :])).astype(jnp.float32)
    ksb = jnp.transpose(ks.reshape(B, n_blk, SEL_BLOCK, G, DH), (0, 3, 1, 2, 4))
    vsb = jnp.transpose(vs.reshape(B, n_blk, SEL_BLOCK, G, DH), (0, 3, 1, 2, 4))
    b_idx = jnp.arange(B)[:, None, None, None]
    g_idx = jnp.arange(G)[None, None, :, None]

    kwp = jnp.pad(kw, ((0, 0), (WINDOW, 0), (0, 0), (0, 0)))
    vwp = jnp.pad(vw, ((0, 0), (WINDOW, 0), (0, 0), (0, 0)))

    def q_block(i):
        s0 = i * NSA_Q_BLOCK
        qb = lax.dynamic_slice_in_dim(q, s0, NSA_Q_BLOCK, axis=1)
        gb = lax.dynamic_slice_in_dim(gates, s0, NSA_Q_BLOCK, axis=1)
        t = s0 + jnp.arange(NSA_Q_BLOCK)

        s_c = jnp.einsum('bqghd,bngd->bqghn', qb, k_cmp)
        m_c = (cmp_end[None, :] <= t[:, None])[None, :, None, None, :]
        p_c = masked_softmax(s_c, m_c)
        o_c = jnp.einsum('bqghn,bngd->bqghd', p_c.astype(v_cmp.dtype), v_cmp)

        imp = jnp.einsum('bqghn,nm->bqgm', p_c, overlap)
        cur = t // SEL_BLOCK
        valid = sel_start[None, :] <= t[:, None]
        forced = (blk[None, :] == 0) | (blk[None, :] == cur[:, None]) | (blk[None, :] == cur[:, None] - 1)
        score = jnp.where(valid[None, :, None, :],
                          imp + jnp.where(forced, FORCE_BONUS, 0.0)[None, :, None, :], -FORCE_BONUS)
        top_v, top_i = lax.top_k(score, n_sel)
        sel_ok = top_v > -1.0
        k_g = ksb[b_idx, g_idx, top_i]
        v_g = vsb[b_idx, g_idx, top_i].reshape(B, NSA_Q_BLOCK, G, n_sel * SEL_BLOCK, DH)
        s_s = jnp.einsum('bqghd,bqgnld->bqghnl', qb, k_g).reshape(B, NSA_Q_BLOCK, G, HPG, n_sel * SEL_BLOCK)
        kpos = (top_i[..., None] * SEL_BLOCK + jnp.arange(SEL_BLOCK)).reshape(B, NSA_Q_BLOCK, G, n_sel * SEL_BLOCK)
        m_s = jnp.repeat(sel_ok, SEL_BLOCK, axis=-1) & (kpos <= t[None, :, None, None])
        p_s = masked_softmax(s_s, m_s[:, :, :, None, :])
        o_s = jnp.einsum('bqghk,bqgkd->bqghd', p_s.astype(v_g.dtype), v_g)

        k_w = lax.dynamic_slice_in_dim(kwp, s0, NSA_Q_BLOCK + WINDOW, axis=1)
        v_w = lax.dynamic_slice_in_dim(vwp, s0, NSA_Q_BLOCK + WINDOW, axis=1)
        kpos_w = s0 - WINDOW + jnp.arange(NSA_Q_BLOCK + WINDOW)
        m_w = ((kpos_w[None, :] <= t[:, None]) & (kpos_w[None, :] > t[:, None] - WINDOW)
               & (kpos_w[None, :] >= 0))[None, :, None, None, :]
        s_w = jnp.einsum('bqghd,bkgd->bqghk', qb, k_w)
        p_w = masked_softmax(s_w, m_w)
        o_w = jnp.einsum('bqghk,bkgd->bqghd', p_w.astype(v_w.dtype), v_w)

        return (gb[:, :, 0, :, :, None] * o_c + gb[:, :, 1, :, :, None] * o_s
                + gb[:, :, 2, :, :, None] * o_w)

    out = lax.map(q_block, jnp.arange(S // NSA_Q_BLOCK))
    return jnp.moveaxis(out, 0, 1).reshape(B, S, Q_WIDTH)


def setup_inputs(seed: int = 0) -> dict:
    key = jax.random.key(seed)
    k = jax.random.split(key, 23)
    L = DEPTH
    f32 = jnp.float32

    def w(kk, shape, fan_in):
        return jax.random.normal(kk, shape, f32) * (fan_in ** -0.5)

    def gain(kk, shape):
        return 1.0 + 0.02 * jax.random.normal(kk, shape, f32)

    return {
        'x': jax.random.normal(k[0], (BATCH, SEQ, D_MODEL), f32),
        'ffn1_norm': gain(k[1], (L, D_MODEL)),
        'ffn1_w_gate': w(k[2], (L, D_MODEL, D_FF), D_MODEL),
        'ffn1_w_up': w(k[3], (L, D_MODEL, D_FF), D_MODEL),
        'ffn1_w_down': w(k[4], (L, D_FF, D_MODEL), D_FF),
        'mix_norm': gain(k[5], (L, D_MODEL)),
        'w_in': w(k[6], (L, D_MODEL, IN_WIDTH), D_MODEL),
        'pool_w': w(k[7], (L, POOL_GROUPS, POOL_GROUP_DIM, POOL_GROUP_DIM), POOL_GROUP_DIM),
        'pool_scale': gain(k[8], (L, POOL_WIDTH)),
        'cmp_pe_k': 0.1 * jax.random.normal(k[9], (L, CMP_BLOCK, HEAD_DIM), f32),
        'cmp_pe_v': 0.1 * jax.random.normal(k[10], (L, CMP_BLOCK, HEAD_DIM), f32),
        'cmp_k_w1': w(k[11], (L, CMP_BLOCK * HEAD_DIM, CMP_HIDDEN), CMP_BLOCK * HEAD_DIM),
        'cmp_k_w2': w(k[12], (L, CMP_HIDDEN, HEAD_DIM), CMP_HIDDEN),
        'cmp_v_w1': w(k[13], (L, CMP_BLOCK * HEAD_DIM, CMP_HIDDEN), CMP_BLOCK * HEAD_DIM),
        'cmp_v_w2': w(k[14], (L, CMP_HIDDEN, HEAD_DIM), CMP_HIDDEN),
        'w_pool_up': w(k[15], (L, POOL_WIDTH, D_MODEL), POOL_WIDTH),
        'w_nsa_up': w(k[16], (L, Q_WIDTH, D_MODEL), Q_WIDTH),
        'w_out': w(k[17], (L, D_MODEL, D_MODEL), D_MODEL),
        'ffn2_norm': gain(k[18], (L, D_MODEL)),
        'ffn2_w_gate': w(k[19], (L, D_MODEL, D_FF), D_MODEL),
        'ffn2_w_up': w(k[20], (L, D_MODEL, D_FF), D_MODEL),
        'ffn2_w_down': w(k[21], (L, D_FF, D_MODEL), D_FF),
        'final_norm': gain(k[22], (D_MODEL,)),
    }


def reference(x, ffn1_norm, ffn1_w_gate, ffn1_w_up, ffn1_w_down, mix_norm, w_in, pool_w, pool_scale,
              cmp_pe_k, cmp_pe_v, cmp_k_w1, cmp_k_w2, cmp_v_w1, cmp_v_w2, w_pool_up, w_nsa_up, w_out,
              ffn2_norm, ffn2_w_gate, ffn2_w_up, ffn2_w_down, final_norm):
    splits = np.cumsum([POOL_WIDTH, Q_WIDTH] + [KV_WIDTH] * 6 + [N_NSA_BRANCHES * N_HEADS, D_MODEL])
    h = x
    for l in range(DEPTH):
        h = h + 0.5 * swiglu(rms_norm(h, ffn1_norm[l]), ffn1_w_gate[l], ffn1_w_up[l], ffn1_w_down[l])
        u = rms_norm(h, mix_norm[l])
        proj = u @ w_in[l]
        (xp, q, kc, vc, ks, vs, kw, vw, nsa_g, g_pool, g_nsa) = jnp.split(proj, splits, axis=-1)
        a = pool_mixer(xp, pool_w[l], pool_scale[l])
        b = nsa_mixer(q, kc, vc, ks, vs, kw, vw, nsa_g, cmp_pe_k[l], cmp_pe_v[l],
                      cmp_k_w1[l], cmp_k_w2[l], cmp_v_w1[l], cmp_v_w2[l])
        merged = jax.nn.sigmoid(g_pool) * (a @ w_pool_up[l]) + jax.nn.sigmoid(g_nsa) * (b @ w_nsa_up[l])
        h = h + merged @ w_out[l]
        h = h + 0.5 * swiglu(rms_norm(h, ffn2_norm[l]), ffn2_w_gate[l], ffn2_w_up[l], ffn2_w_down[l])
    return rms_norm(h, final_norm)
```

```python
import functools

import jax
import jax.numpy as jnp
from jax import lax
from jax.experimental import pallas as pl
from jax.experimental.pallas import tpu as pltpu

F32 = jnp.float32
BF16 = jnp.bfloat16

POOL_WINDOWS = (2, 4, 8, 16)
POOL_GROUP_DIM = 256
N_HEADS = 16
N_KV_GROUPS = 4
HEADS_PER_GROUP = N_HEADS // N_KV_GROUPS
HEAD_DIM = 128
CMP_BLOCK = 32
CMP_STRIDE = 16
SEL_BLOCK = 64
SEL_TOPK = 16
WINDOW = 512
N_NSA_BRANCHES = 3
FORCE_BONUS = 1000.0
RMS_EPS = 1e-6

LANES = 128
MASK_BIAS = -1e30
VMEM_LIMIT = 56 * 1024 * 1024


def _dot(a, b):
    return jnp.dot(a, b, preferred_element_type=F32)


def _dot_nt(a, b):
    return lax.dot_general(a, b, (((1,), (1,)), ((), ())), preferred_element_type=F32)


def _rms(x, g):
    ms = jnp.mean(x * x, axis=-1, keepdims=True)
    return x * lax.rsqrt(ms + RMS_EPS) * g


def _params(*sem):
    return pltpu.CompilerParams(dimension_semantics=sem, vmem_limit_bytes=VMEM_LIMIT)


def _ffn_kernel(*refs, final):
    if final:
        x_ref, g_ref, wg_ref, wu_ref, wd_ref, fg_ref, o_ref, xn_ref, acc_ref = refs
    else:
        x_ref, g_ref, wg_ref, wu_ref, wd_ref, o_ref, xn_ref, acc_ref = refs
    j = pl.program_id(1)

    @pl.when(j == 0)
    def _():
        xn_ref[...] = _rms(x_ref[...], g_ref[...]).astype(BF16)
        acc_ref[...] = jnp.zeros_like(acc_ref)

    xn = xn_ref[...]
    gate = _dot(xn, wg_ref[...])
    up = _dot(xn, wu_ref[...])
    act = (gate * jax.nn.sigmoid(gate) * up).astype(BF16)
    acc_ref[...] += _dot(act, wd_ref[...])

    @pl.when(j == pl.num_programs(1) - 1)
    def _():
        h = x_ref[...] + 0.5 * acc_ref[...]
        if final:
            h = _rms(h, fg_ref[...])
        o_ref[...] = h


def _ffn(x, g, wg, wu, wd, final_g=None, *, tm=512, tf=512):
    T, D = x.shape
    F = wg.shape[1]
    assert T % tm == 0 and F % tf == 0
    final = final_g is not None
    in_specs = [
        pl.BlockSpec((tm, D), lambda i, j: (i, 0)),
        pl.BlockSpec((1, D), lambda i, j: (0, 0)),
        pl.BlockSpec((D, tf), lambda i, j: (0, j)),
        pl.BlockSpec((D, tf), lambda i, j: (0, j)),
        pl.BlockSpec((tf, D), lambda i, j: (j, 0)),
    ]
    args = [x, g.reshape(1, D), wg, wu, wd]
    if final:
        in_specs.append(pl.BlockSpec((1, D), lambda i, j: (0, 0)))
        args.append(final_g.reshape(1, D))
    return pl.pallas_call(
        functools.partial(_ffn_kernel, final=final),
        grid=(T // tm, F // tf),
        in_specs=in_specs,
        out_specs=pl.BlockSpec((tm, D), lambda i, j: (i, 0)),
        out_shape=jax.ShapeDtypeStruct((T, D), F32),
        scratch_shapes=[pltpu.VMEM((tm, D), BF16), pltpu.VMEM((tm, D), F32)],
        compiler_params=_params("parallel", "arbitrary"),
        name="ffn_final" if final else "ffn",
    )(*args)


def _proj_kernel(x_ref, g_ref, w_ref, o_ref, xn_ref):
    @pl.when(pl.program_id(1) == 0)
    def _():
        xn_ref[...] = _rms(x_ref[...], g_ref[...]).astype(BF16)

    o_ref[...] = _dot(xn_ref[...], w_ref[...]).astype(o_ref.dtype)


def _proj(x, g, w, *, tm=1024, tn=1536):
    T, D = x.shape
    N = w.shape[1]
    tm = min(tm, T)
    assert T % tm == 0 and N % tn == 0
    return pl.pallas_call(
        _proj_kernel,
        grid=(T // tm, N // tn),
        in_specs=[
            pl.BlockSpec((tm, D), lambda i, j: (i, 0)),
            pl.BlockSpec((1, D), lambda i, j: (0, 0)),
            pl.BlockSpec((D, tn), lambda i, j: (0, j)),
        ],
        out_specs=pl.BlockSpec((tm, tn), lambda i, j: (i, j)),
        out_shape=jax.ShapeDtypeStruct((T, N), BF16),
        scratch_shapes=[pltpu.VMEM((tm, D), BF16)],
        compiler_params=_params("parallel", "arbitrary"),
        name="in_proj",
    )(x, g.reshape(1, D), w)


def _cmp_kernel(c_ref, w1_ref, w2_ref, pe_ref, o_ref):
    c = c_ref[...]
    nch, half = c.shape
    a = _dot(c, w1_ref[:half, :])
    b = _dot(c, w1_ref[half:, :])
    pe = jnp.broadcast_to(pe_ref[...], (8, 2 * half))
    pe_term = _dot(pe, w1_ref[...])[0:1]
    pre = a + pltpu.roll(b, nch - 1, 0) + pe_term
    hid = (pre * jax.nn.sigmoid(pre)).astype(BF16)
    o_ref[...] = _dot(hid, w2_ref[...]).astype(o_ref.dtype)


def _compress(ckv, w1, w2, pe):
    _, B, G, nch, cw = ckv.shape
    hid = w1.shape[2]
    return pl.pallas_call(
        _cmp_kernel,
        grid=(2, B, G),
        in_specs=[
            pl.BlockSpec((None, None, None, nch, cw), lambda s, b, g: (s, b, g, 0, 0)),
            pl.BlockSpec((None, 2 * cw, hid), lambda s, b, g: (s, 0, 0)),
            pl.BlockSpec((None, hid, HEAD_DIM), lambda s, b, g: (s, 0, 0)),
            pl.BlockSpec((None, 1, 2 * cw), lambda s, b, g: (s, 0, 0)),
        ],
        out_specs=pl.BlockSpec((None, None, None, nch, HEAD_DIM), lambda s, b, g: (s, b, g, 0, 0)),
        out_shape=jax.ShapeDtypeStruct((2, B, G, nch, HEAD_DIM), BF16),
        compiler_params=_params("parallel", "parallel", "parallel"),
        name="compress",
    )(ckv, w1, w2, pe)


def _nsa_kernel(q_ref, gl_ref, kc_ref, vc_ref, ov_ref, ka_ref, ve_ref, kw_ref, vw_ref, o_ref,
                qa_ref, m_ref, acc_ref, *, tq, tk, n_sel):
    hpg, dh = HEADS_PER_GROUP, HEAD_DIM
    rows = hpg * tq
    s0 = pl.program_id(2) * tq
    t_col = s0 + lax.broadcasted_iota(jnp.int32, (tq, 1), 0)

    for h in range(hpg):
        qa_ref[h * tq:(h + 1) * tq, 0:dh] = q_ref[:, h * dh:(h + 1) * dh]
    qh = qa_ref[:, 0:dh]

    nk = kc_ref.shape[0]
    s_c = _dot_nt(qh, kc_ref[...]).reshape(hpg, tq, nk)
    n_idx = lax.broadcasted_iota(jnp.int32, (tq, nk), 1)
    mask_c = (n_idx * CMP_STRIDE + (CMP_BLOCK - 1)) <= t_col
    s_c = jnp.where(mask_c[None], s_c, -jnp.inf)
    m_c = jnp.max(s_c, axis=-1, keepdims=True)
    m_c = jnp.where(m_c == -jnp.inf, 0.0, m_c)
    p_c = jnp.exp(s_c - m_c)
    p_c = p_c / jnp.maximum(jnp.sum(p_c, axis=-1, keepdims=True), 1e-30)
    o_c = _dot(p_c.reshape(rows, nk).astype(BF16), vc_ref[...])

    p_sum = p_c[0]
    for h in range(1, hpg):
        p_sum = p_sum + p_c[h]
    p_hi = p_sum.astype(BF16)
    p_lo = (p_sum - p_hi.astype(F32)).astype(BF16)
    imp = _dot(p_hi, ov_ref[...]) + _dot(p_lo, ov_ref[...])
    blk = lax.broadcasted_iota(jnp.int32, (tq, LANES), 1)
    cur = lax.shift_right_logical(t_col, SEL_BLOCK.bit_length() - 1)
    valid = blk * SEL_BLOCK <= t_col
    forced = (blk == 0) | (blk == cur) | (blk == cur - 1)
    score = jnp.where(valid, imp + jnp.where(forced, FORCE_BONUS, 0.0), -FORCE_BONUS)
    sc_t = score.T
    ridx = lax.broadcasted_iota(jnp.int32, (LANES, tq), 0).astype(F32)
    sel_t = jnp.zeros((LANES, tq), F32)
    for _ in range(n_sel):
        mx = jnp.max(sc_t, axis=0, keepdims=True)
        idx = jnp.min(jnp.where(sc_t == mx, ridx, float(LANES)), axis=0, keepdims=True)
        pick = ridx == idx
        sel_t = jnp.where(pick & (mx > -1.0), 1.0, sel_t)
        sc_t = jnp.where(pick, -jnp.inf, sc_t)
    bias = jnp.where(sel_t.T > 0.5, 0.0, MASK_BIAS).astype(BF16)
    for h in range(hpg):
        qa_ref[h * tq:(h + 1) * tq, dh:2 * dh] = bias

    qa = qa_ref[...]
    m_ref[...] = jnp.full_like(m_ref, MASK_BIAS)
    acc_ref[...] = jnp.zeros_like(acc_ref)

    def sel_step(j, causal):
        koff = pl.multiple_of(j * tk, tk)
        s = _dot_nt(qa, ka_ref[pl.ds(koff, tk), :])
        if causal:
            kpos = koff + lax.broadcasted_iota(jnp.int32, (tq, tk), 1)
            s = jnp.where((kpos <= t_col)[None], s.reshape(hpg, tq, tk), MASK_BIAS).reshape(rows, tk)
        m_prev = m_ref[...]
        m_new = jnp.maximum(m_prev, jnp.max(s, axis=-1, keepdims=True))
        alpha = jnp.exp(m_prev - m_new)
        p = jnp.exp(s - pltpu.repeat(m_new, tk // LANES, axis=1))
        pv = _dot(p.astype(BF16), ve_ref[pl.ds(koff, tk), :])
        acc_ref[...] = acc_ref[...] * pltpu.repeat(alpha, 2, axis=1) + pv
        m_ref[...] = m_new

    j_last = s0 // tk

    def body(j, carry):
        sel_step(j, False)
        return carry

    lax.fori_loop(0, j_last, body, 0)
    sel_step(j_last, True)
    acc = acc_ref[...]
    o_s = acc[:, 0:dh] / acc[:, dh:dh + 1]

    wk = WINDOW + tq
    start = pl.multiple_of(jnp.maximum(s0 - WINDOW, 0), tq)
    s_w = _dot_nt(qh, kw_ref[pl.ds(start, wk), :]).reshape(hpg, tq, wk)
    kpos = start + lax.broadcasted_iota(jnp.int32, (tq, wk), 1)
    mask_w = (kpos <= t_col) & (kpos > t_col - WINDOW)
    s_w = jnp.where(mask_w[None], s_w, -jnp.inf)
    p_w = jnp.exp(s_w - jnp.max(s_w, axis=-1, keepdims=True))
    l_w = jnp.sum(p_w, axis=-1, keepdims=True).reshape(rows, 1)
    o_w = _dot(p_w.reshape(rows, wk).astype(BF16), vw_ref[pl.ds(start, wk), :]) / l_w

    gates = jax.nn.sigmoid(gl_ref[...].astype(F32))
    for h in range(hpg):
        r = slice(h * tq, (h + 1) * tq)
        out = (gates[:, h:h + 1] * o_c[r]
               + gates[:, hpg + h:hpg + h + 1] * o_s[r]
               + gates[:, 2 * hpg + h:2 * hpg + h + 1] * o_w[r])
        o_ref[:, h * dh:(h + 1) * dh] = out.astype(o_ref.dtype)


def _nsa(P, kcmp, vcmp, overlap, k_aug, v_ext, lay, *, B, S, tq=256, tk=512):
    G, dh, hpg = N_KV_GROUPS, HEAD_DIM, HEADS_PER_GROUP
    tq = min(tq, S)
    tk = min(tk, S)
    nk = kcmp.shape[2]
    n_sel = min(SEL_TOPK, S // SEL_BLOCK)
    assert S % tq == 0 and tk % tq == 0 and S % tk == 0 and S >= WINDOW + tq
    assert S // SEL_BLOCK <= LANES and nk % LANES == 0
    nq = S // tq
    qw = hpg * dh
    kern = functools.partial(_nsa_kernel, tq=tq, tk=tk, n_sel=n_sel)
    return pl.pallas_call(
        kern,
        grid=(B, G, nq),
        in_specs=[
            pl.BlockSpec((tq, qw), lambda b, g, i: (b * nq + i, lay["q"] // qw + g)),
            pl.BlockSpec((tq, LANES), lambda b, g, i: (b * nq + i, lay["gates"] // LANES + g)),
            pl.BlockSpec((None, None, nk, dh), lambda b, g, i: (b, g, 0, 0)),
            pl.BlockSpec((None, None, nk, dh), lambda b, g, i: (b, g, 0, 0)),
            pl.BlockSpec((nk, LANES), lambda b, g, i: (0, 0)),
            pl.BlockSpec((None, None, S, 2 * dh), lambda b, g, i: (b, g, 0, 0)),
            pl.BlockSpec((None, None, S, 2 * dh), lambda b, g, i: (b, g, 0, 0)),
            pl.BlockSpec((S, dh), lambda b, g, i: (b, lay["kw"] // dh + g)),
            pl.BlockSpec((S, dh), lambda b, g, i: (b, lay["vw"] // dh + g)),
        ],
        out_specs=pl.BlockSpec((tq, qw), lambda b, g, i: (b * nq + i, g)),
        out_shape=jax.ShapeDtypeStruct((B * S, G * qw), BF16),
        scratch_shapes=[
            pltpu.VMEM((hpg * tq, 2 * dh), BF16),
            pltpu.VMEM((hpg * tq, LANES), F32),
            pltpu.VMEM((hpg * tq, 2 * dh), F32),
        ],
        compiler_params=_params("parallel", "parallel", "arbitrary"),
        name="nsa",
    )(P, P, kcmp, vcmp, overlap, k_aug, v_ext, P, P)


def _mix_kernel(xp_ref, halo_ref, b_ref, gp_ref, gn_ref, pw_ref, ps_ref, wpu_ref, wnu_ref, o_ref, *, tm, S):
    halo_rows = halo_ref.shape[0]
    gd = POOL_GROUP_DIM
    t0 = (pl.program_id(0) * tm) % S
    x = xp_ref[...].astype(F32)
    halo = halo_ref[...].astype(F32) * (t0 > 0).astype(F32)
    xc = jnp.concatenate([halo, x], axis=0)
    tpos = t0 + lax.broadcasted_iota(jnp.int32, (tm, 1), 0)
    a_up = jnp.zeros((tm, o_ref.shape[1]), F32)
    for gi, win in enumerate(POOL_WINDOWS):
        cols = slice(gi * gd, (gi + 1) * gd)
        s = xc[:, cols]
        d = 1
        while d < win:
            s = s + pltpu.roll(s, d, 0)
            d *= 2
        cnt = jnp.minimum(tpos + 1, win).astype(F32)
        pooled = s[halo_rows:] / cnt - x[:, cols]
        y = _dot(pooled.astype(BF16), pw_ref[gi]) * ps_ref[:, cols]
        a_up = a_up + _dot(y.astype(BF16), wpu_ref[cols, :])
    b_up = _dot(b_ref[...], wnu_ref[...])
    merged = (jax.nn.sigmoid(gp_ref[...].astype(F32)) * a_up
              + jax.nn.sigmoid(gn_ref[...].astype(F32)) * b_up)
    o_ref[...] = merged.astype(o_ref.dtype)


def _mix(P, b_attn, pool_w, pool_scale, w_pool_up, w_nsa_up, lay, *, S, tm=256):
    T = P.shape[0]
    pw = pool_w.shape[0] * pool_w.shape[1]
    D = w_pool_up.shape[1]
    halo = max(POOL_WINDOWS)
    assert S % tm == 0 and tm % halo == 0 and lay["pool"] == 0
    kern = functools.partial(_mix_kernel, tm=tm, S=S)
    return pl.pallas_call(
        kern,
        grid=(T // tm,),
        in_specs=[
            pl.BlockSpec((tm, pw), lambda i: (i, 0)),
            pl.BlockSpec((halo, pw), lambda i: (jnp.maximum(i * (tm // halo) - 1, 0), 0)),
            pl.BlockSpec((tm, b_attn.shape[1]), lambda i: (i, 0)),
            pl.BlockSpec((tm, D), lambda i: (i, lay["g_pool"] // D)),
            pl.BlockSpec((tm, D), lambda i: (i, lay["g_nsa"] // D)),
            pl.BlockSpec(pool_w.shape, lambda i: (0, 0, 0)),
            pl.BlockSpec((1, pw), lambda i: (0, 0)),
            pl.BlockSpec(w_pool_up.shape, lambda i: (0, 0)),
            pl.BlockSpec(w_nsa_up.shape, lambda i: (0, 0)),
        ],
        out_specs=pl.BlockSpec((tm, D), lambda i: (i, 0)),
        out_shape=jax.ShapeDtypeStruct((T, D), BF16),
        compiler_params=_params("parallel"),
        name="mix",
    )(P, P, b_attn, P, P, pool_w, pool_scale.reshape(1, pw), w_pool_up, w_nsa_up)


def _outproj_kernel(h_ref, m_ref, w_ref, o_ref):
    o_ref[...] = h_ref[...] + _dot(m_ref[...], w_ref[...])


def _outproj(h, merged, w_out, *, tm=512):
    T, D = h.shape
    assert T % tm == 0
    return pl.pallas_call(
        _outproj_kernel,
        grid=(T // tm,),
        in_specs=[
            pl.BlockSpec((tm, D), lambda i: (i, 0)),
            pl.BlockSpec((tm, D), lambda i: (i, 0)),
            pl.BlockSpec(w_out.shape, lambda i: (0, 0)),
        ],
        out_specs=pl.BlockSpec((tm, D), lambda i: (i, 0)),
        out_shape=jax.ShapeDtypeStruct((T, D), F32),
        compiler_params=_params("parallel"),
        name="out_proj",
    )(h, merged, w_out)


def _prep_w_in(w_in, pool_width, d_model):
    G, hpg, dh = N_KV_GROUPS, HEADS_PER_GROUP, HEAD_DIM
    qw, kvw = N_HEADS * dh, G * dh
    o_q = pool_width
    o_kv = o_q + qw
    o_gate = o_kv + 6 * kvw
    o_gp = o_gate + N_NSA_BRANCHES * N_HEADS
    o_gn = o_gp + d_model
    assert w_in.shape[1] == o_gn + d_model
    gate_cols = w_in[:, o_gate:o_gp].reshape(-1, N_NSA_BRANCHES, G, hpg)
    gate_cols = jnp.transpose(gate_cols, (0, 2, 1, 3)).reshape(-1, G, N_NSA_BRANCHES * hpg)
    gate_cols = jnp.pad(gate_cols, ((0, 0), (0, 0), (0, LANES - N_NSA_BRANCHES * hpg))).reshape(-1, G * LANES)
    w = jnp.concatenate([
        w_in[:, :o_q],
        w_in[:, o_q:o_kv] * (dh ** -0.5),
        w_in[:, o_kv:o_gate],
        w_in[:, o_gp:o_gn],
        w_in[:, o_gn:],
        gate_cols,
    ], axis=1).astype(BF16)
    lay = {"pool": 0, "q": o_q, "kc": o_kv, "vc": o_kv + kvw, "ks": o_kv + 2 * kvw, "vs": o_kv + 3 * kvw,
           "kw": o_kv + 4 * kvw, "vw": o_kv + 5 * kvw, "g_pool": o_gate, "g_nsa": o_gate + d_model,
           "gates": o_gate + 2 * d_model}
    return w, lay


def _overlap_matrix(nk, n_cmp, n_blk):
    n = jnp.arange(nk)[:, None]
    m = jnp.arange(LANES)[None, :]
    cs = n * CMP_STRIDE
    ss = m * SEL_BLOCK
    ov = (cs <= ss + SEL_BLOCK - 1) & (cs + CMP_BLOCK - 1 >= ss) & (n < n_cmp) & (m < n_blk)
    return ov.astype(BF16)


def kernel(x, ffn1_norm, ffn1_w_gate, ffn1_w_up, ffn1_w_down, mix_norm, w_in, pool_w, pool_scale,
           cmp_pe_k, cmp_pe_v, cmp_k_w1, cmp_k_w2, cmp_v_w1, cmp_v_w2, w_pool_up, w_nsa_up, w_out,
           ffn2_norm, ffn2_w_gate, ffn2_w_up, ffn2_w_down, final_norm):
    B, S, D = x.shape
    depth = ffn1_norm.shape[0]
    G, dh = N_KV_GROUPS, HEAD_DIM
    T = B * S
    pool_width = pool_w.shape[1] * pool_w.shape[2]
    n_chunk = S // CMP_STRIDE
    n_cmp = n_chunk - CMP_BLOCK // CMP_STRIDE + 1
    n_blk = S // SEL_BLOCK
    cw = CMP_STRIDE * dh

    h = x.reshape(T, D)
    for l in range(depth):
        last = l == depth - 1
        h = _ffn(h, ffn1_norm[l], ffn1_w_gate[l].astype(BF16), ffn1_w_up[l].astype(BF16),
                 ffn1_w_down[l].astype(BF16))

        w_p, lay = _prep_w_in(w_in[l], pool_width, D)
        P = _proj(h, mix_norm[l], w_p)

        def group_major(name):
            cols = P[:, lay[name]:lay[name] + G * dh]
            return jnp.transpose(cols.reshape(B, S, G, dh), (0, 2, 1, 3))

        ckv = jnp.stack([group_major("kc"), group_major("vc")]).reshape(2, B, G, n_chunk, cw)
        w1 = jnp.stack([cmp_k_w1[l], cmp_v_w1[l]]).astype(BF16)
        w2 = jnp.stack([cmp_k_w2[l], cmp_v_w2[l]]).astype(BF16)
        pe = jnp.stack([cmp_pe_k[l], cmp_pe_v[l]]).reshape(2, 1, CMP_BLOCK * dh).astype(BF16)
        cmp_kv = _compress(ckv, w1, w2, pe)

        blk_onehot = (jnp.arange(S)[:, None] // SEL_BLOCK == jnp.arange(LANES)[None, :]).astype(BF16)
        k_aug = jnp.concatenate([group_major("ks"), jnp.broadcast_to(blk_onehot, (B, G, S, LANES))], axis=-1)
        ones_col = (jnp.arange(LANES) == 0).astype(BF16)
        v_ext = jnp.concatenate([group_major("vs"), jnp.broadcast_to(ones_col, (B, G, S, LANES))], axis=-1)
        overlap = _overlap_matrix(n_chunk, n_cmp, n_blk)

        b_attn = _nsa(P, cmp_kv[0], cmp_kv[1], overlap, k_aug, v_ext, lay, B=B, S=S)
        merged = _mix(P, b_attn, pool_w[l].astype(BF16), pool_scale[l], w_pool_up[l].astype(BF16),
                      w_nsa_up[l].astype(BF16), lay, S=S)
        h = _outproj(h, merged, w_out[l].astype(BF16))
        h = _ffn(h, ffn2_norm[l], ffn2_w_gate[l].astype(BF16), ffn2_w_up[l].astype(BF16),
                 ffn2_w_down[l].astype(BF16), final_norm if last else None)
    if depth == 0:
        raise ValueError("depth must be positive")
    return h.reshape(B, S, D)
```

```python
import functools

import jax
import jax.numpy as jnp
from jax import lax
from jax.experimental import pallas as pl
from jax.experimental.pallas import tpu as pltpu

F32 = jnp.float32
BF16 = jnp.bfloat16

POOL_WINDOWS = (2, 4, 8, 16)
POOL_GROUP_DIM = 256
N_HEADS = 16
N_KV_GROUPS = 4
HEADS_PER_GROUP = N_HEADS // N_KV_GROUPS
HEAD_DIM = 128
CMP_BLOCK = 32
CMP_STRIDE = 16
SEL_BLOCK = 64
SEL_TOPK = 16
WINDOW = 512
N_NSA_BRANCHES = 3
FORCE_BONUS = 1000.0
RMS_EPS = 1e-6

LOG2E = 1.4426950408889634
LANES = 128
MASK_BIAS = -1e30
VMEM_LIMIT = 56 * 1024 * 1024


def _dot(a, b):
    return jnp.dot(a, b, preferred_element_type=F32)


def _dot_nt(a, b):
    return lax.dot_general(a, b, (((1,), (1,)), ((), ())), preferred_element_type=F32)


def _rms(x, g):
    ms = jnp.mean(x * x, axis=-1, keepdims=True)
    return x * lax.rsqrt(ms + RMS_EPS) * g


def _params(*sem):
    return pltpu.CompilerParams(dimension_semantics=sem, vmem_limit_bytes=VMEM_LIMIT)


def _ffn_kernel(*refs, final):
    if final:
        x_ref, g_ref, wg_ref, wu_ref, wd_ref, fg_ref, o_ref, xn_ref = refs
    else:
        x_ref, g_ref, wg_ref, wu_ref, wd_ref, o_ref, xn_ref = refs
    j = pl.program_id(1)

    @pl.when(j == 0)
    def _():
        xn_ref[...] = _rms(x_ref[...], g_ref[...]).astype(BF16)
        o_ref[...] = jnp.zeros_like(o_ref)

    xn = xn_ref[...]
    gate = _dot(xn, wg_ref[...])
    up = _dot(xn, wu_ref[...])
    act = (gate * jax.nn.sigmoid(gate) * up).astype(BF16)
    o_ref[...] += _dot(act, wd_ref[...])

    @pl.when(j == pl.num_programs(1) - 1)
    def _():
        h = x_ref[...] + 0.5 * o_ref[...]
        if final:
            h = _rms(h, fg_ref[...])
        o_ref[...] = h


def _ffn(x, g, wg, wu, wd, final_g=None, *, tm=1024, tf=512):
    T, D = x.shape
    F = wg.shape[1]
    assert T % tm == 0 and F % tf == 0
    final = final_g is not None
    in_specs = [
        pl.BlockSpec((tm, D), lambda i, j: (i, 0), pipeline_mode=pl.Buffered(1)),
        pl.BlockSpec((1, D), lambda i, j: (0, 0)),
        pl.BlockSpec((D, tf), lambda i, j: (0, j)),
        pl.BlockSpec((D, tf), lambda i, j: (0, j)),
        pl.BlockSpec((tf, D), lambda i, j: (j, 0)),
    ]
    args = [x, g.reshape(1, D), wg, wu, wd]
    if final:
        in_specs.append(pl.BlockSpec((1, D), lambda i, j: (0, 0)))
        args.append(final_g.reshape(1, D))
    return pl.pallas_call(
        functools.partial(_ffn_kernel, final=final),
        grid=(T // tm, F // tf),
        in_specs=in_specs,
        out_specs=pl.BlockSpec((tm, D), lambda i, j: (i, 0)),
        out_shape=jax.ShapeDtypeStruct((T, D), F32),
        scratch_shapes=[pltpu.VMEM((tm, D), BF16)],
        compiler_params=_params("parallel", "arbitrary"),
        name="ffn_final" if final else "ffn",
    )(*args)


def _proj_kernel(x_ref, g_ref, w_ref, o_ref, xn_ref):
    @pl.when(pl.program_id(1) == 0)
    def _():
        xn_ref[...] = _rms(x_ref[...], g_ref[...]).astype(BF16)

    o_ref[...] = _dot(xn_ref[...], w_ref[...]).astype(o_ref.dtype)


def _proj(x, g, w, *, tm=1024, tn=1536):
    T, D = x.shape
    N = w.shape[1]
    tm = min(tm, T)
    assert T % tm == 0 and N % tn == 0
    return pl.pallas_call(
        _proj_kernel,
        grid=(T // tm, N // tn),
        in_specs=[
            pl.BlockSpec((tm, D), lambda i, j: (i, 0)),
            pl.BlockSpec((1, D), lambda i, j: (0, 0)),
            pl.BlockSpec((D, tn), lambda i, j: (0, j)),
        ],
        out_specs=pl.BlockSpec((tm, tn), lambda i, j: (i, j)),
        out_shape=jax.ShapeDtypeStruct((T, N), BF16),
        scratch_shapes=[pltpu.VMEM((tm, D), BF16)],
        compiler_params=_params("parallel", "arbitrary"),
        name="in_proj",
    )(x, g.reshape(1, D), w)


def _cmp_kernel(c_ref, w1_ref, w2_ref, pe_ref, o_ref):
    c = c_ref[...]
    nch, half = c.shape
    a = _dot(c, w1_ref[:half, :])
    b = _dot(c, w1_ref[half:, :])
    pe = jnp.broadcast_to(pe_ref[...], (8, 2 * half))
    pe_term = _dot(pe, w1_ref[...])[0:1]
    pre = a + pltpu.roll(b, nch - 1, 0) + pe_term
    hid = (pre * jax.nn.sigmoid(pre)).astype(BF16)
    o_ref[...] = _dot(hid, w2_ref[...]).astype(o_ref.dtype)


def _compress(ckv, w1, w2, pe):
    _, B, G, nch, cw = ckv.shape
    hid = w1.shape[2]
    return pl.pallas_call(
        _cmp_kernel,
        grid=(2, B, G),
        in_specs=[
            pl.BlockSpec((None, None, None, nch, cw), lambda s, b, g: (s, b, g, 0, 0)),
            pl.BlockSpec((None, 2 * cw, hid), lambda s, b, g: (s, 0, 0)),
            pl.BlockSpec((None, hid, HEAD_DIM), lambda s, b, g: (s, 0, 0)),
            pl.BlockSpec((None, 1, 2 * cw), lambda s, b, g: (s, 0, 0)),
        ],
        out_specs=pl.BlockSpec((None, None, None, nch, HEAD_DIM), lambda s, b, g: (s, b, g, 0, 0)),
        out_shape=jax.ShapeDtypeStruct((2, B, G, nch, HEAD_DIM), BF16),
        compiler_params=_params("parallel", "parallel", "parallel"),
        name="compress",
    )(ckv, w1, w2, pe)


def _nsa_kernel(q_ref, gl_ref, kc_ref, vc_ref, ov_ref, ks_ref, vs_ref, kw_ref, vw_ref, o_ref,
                ka_ref, ve_ref, qa_ref, m_ref, acc_ref, s_ref, imp_ref, bt_ref, *, tq, tk, n_sel):
    hpg, dh = HEADS_PER_GROUP, HEAD_DIM
    s0 = pl.program_id(2) * tq
    t_col = s0 + lax.broadcasted_iota(jnp.int32, (tq, 1), 0)

    @pl.when(pl.program_id(2) == 0)
    def _():
        seq = ks_ref.shape[0]
        for c in range(seq // tk):
            r = slice(c * tk, (c + 1) * tk)
            kblk = lax.shift_right_logical(c * tk + lax.broadcasted_iota(jnp.int32, (tk, LANES), 0),
                                           SEL_BLOCK.bit_length() - 1)
            lane = lax.broadcasted_iota(jnp.int32, (tk, LANES), 1)
            ka_ref[r, 0:dh] = ks_ref[r, :]
            ka_ref[r, dh:2 * dh] = (kblk == lane).astype(BF16)
            ve_ref[r, 0:dh] = vs_ref[r, :]
            ve_ref[r, dh:2 * dh] = (lane == 0).astype(BF16)

    for h in range(hpg):
        qa_ref[h, :, 0:dh] = q_ref[:, h * dh:(h + 1) * dh]

    nk = kc_ref.shape[0]
    n_idx = lax.broadcasted_iota(jnp.int32, (tq, nk), 1)
    mask_c = (n_idx * CMP_STRIDE + (CMP_BLOCK - 1)) <= t_col
    o_c = []
    p_sum = None
    for h in range(hpg):
        s_c = jnp.where(mask_c, _dot_nt(qa_ref[h, :, 0:dh], kc_ref[...]), -jnp.inf)
        m_c = jnp.max(s_c, axis=-1, keepdims=True)
        m_c = jnp.where(m_c == -jnp.inf, 0.0, m_c)
        p_c = jnp.exp2(s_c - m_c)
        p_c = p_c * (1.0 / jnp.maximum(jnp.sum(p_c, axis=-1, keepdims=True), 1e-30))
        o_c.append(_dot(p_c.astype(BF16), vc_ref[...]))
        p_sum = p_c if p_sum is None else p_sum + p_c

    p_hi = p_sum.astype(BF16)
    p_lo = (p_sum - p_hi.astype(F32)).astype(BF16)
    imp_t = (_dot(p_hi, ov_ref[...]) + _dot(p_lo, ov_ref[...])).T

    imp_ref[...] = imp_t
    bt_ref[...] = jnp.full_like(bt_ref, MASK_BIAS)

    def select_rows(nrows):
        blk = lax.broadcasted_iota(jnp.int32, (nrows, tq), 0)
        t_row = s0 + lax.broadcasted_iota(jnp.int32, (nrows, tq), 1)
        cur = lax.shift_right_logical(t_row, SEL_BLOCK.bit_length() - 1)
        forced = (blk == 0) | (blk == cur) | (blk == cur - 1)
        cand = jnp.where(forced | (blk * SEL_BLOCK > t_row), -FORCE_BONUS, imp_ref[0:nrows, :])
        ridx = blk.astype(F32)
        sc = cand
        for _ in range(n_sel - 3):
            mx = jnp.max(sc, axis=0, keepdims=True)
            idx = jnp.min(jnp.where(sc == mx, ridx, float(LANES)), axis=0, keepdims=True)
            sc = jnp.where(ridx == idx, -jnp.inf, sc)
        sel = forced | ((sc == -jnp.inf) & (cand > -1.0))
        bt_ref[0:nrows, :] = jnp.where(sel, 0.0, MASK_BIAS)

    n_live = (s0 + tq) // SEL_BLOCK
    lo, mid = LANES // 4, LANES // 2
    pl.when(n_live <= lo)(lambda: select_rows(lo))
    pl.when((n_live > lo) & (n_live <= mid))(lambda: select_rows(mid))
    pl.when(n_live > mid)(lambda: select_rows(LANES))
    bias = bt_ref[...].T.astype(BF16)
    for h in range(hpg):
        qa_ref[h, :, dh:2 * dh] = bias

    m_ref[...] = jnp.full_like(m_ref, MASK_BIAS)
    acc_ref[...] = jnp.zeros_like(acc_ref)

    def scores(j, h):
        return _dot_nt(qa_ref[h], ka_ref[pl.ds(pl.multiple_of(j * tk, tk), tk), :])

    def consume(s, j, h):
        m_prev = m_ref[h]
        m_new = jnp.maximum(m_prev, jnp.max(s, axis=-1, keepdims=True))
        alpha = jnp.exp2(m_prev - m_new)
        p = jnp.exp2(s - pltpu.repeat(m_new, tk // LANES, axis=1))
        pv = _dot(p.astype(BF16), ve_ref[pl.ds(pl.multiple_of(j * tk, tk), tk), :])
        acc_ref[h] = acc_ref[h] * pltpu.repeat(alpha, 2, axis=1) + pv
        m_ref[h] = m_new

    for h in range(hpg):
        s_ref[h] = scores(0, h)
    j_last = s0 // tk

    def body(j, carry):
        for h in range(hpg):
            s_next = scores(j + 1, h)
            consume(s_ref[h], j, h)
            s_ref[h] = s_next
        return carry

    lax.fori_loop(0, j_last, body, 0)
    causal = j_last * tk + lax.broadcasted_iota(jnp.int32, (tq, tk), 1) <= t_col
    for h in range(hpg):
        consume(jnp.where(causal, s_ref[h], MASK_BIAS), j_last, h)

    wk = WINDOW + tq
    start = pl.multiple_of(jnp.maximum(s0 - WINDOW, 0), tq)
    kpos = start + lax.broadcasted_iota(jnp.int32, (tq, wk), 1)
    mask_w = (kpos <= t_col) & (kpos > t_col - WINDOW)
    gates = jax.nn.sigmoid(gl_ref[...].astype(F32))
    for h in range(hpg):
        s_w = jnp.where(mask_w, _dot_nt(qa_ref[h, :, 0:dh], kw_ref[pl.ds(start, wk), :]), -jnp.inf)
        p_w = jnp.exp2(s_w - jnp.max(s_w, axis=-1, keepdims=True))
        r_w = 1.0 / jnp.sum(p_w, axis=-1, keepdims=True)
        o_w = _dot(p_w.astype(BF16), vw_ref[pl.ds(start, wk), :]) * r_w
        acc = acc_ref[h]
        o_s = acc[:, 0:dh] * (1.0 / acc[:, dh:dh + 1])
        out = (gates[:, h:h + 1] * o_c[h]
               + gates[:, hpg + h:hpg + h + 1] * o_s
               + gates[:, 2 * hpg + h:2 * hpg + h + 1] * o_w)
        o_ref[:, h * dh:(h + 1) * dh] = out.astype(o_ref.dtype)


def _nsa(P, kcmp, vcmp, overlap, lay, *, B, S, tq=256, tk=512):
    G, dh, hpg = N_KV_GROUPS, HEAD_DIM, HEADS_PER_GROUP
    tq = min(tq, S)
    tk = min(tk, S)
    nk = kcmp.shape[2]
    n_sel = min(SEL_TOPK, S // SEL_BLOCK)
    assert S % tq == 0 and tk % tq == 0 and S % tk == 0 and S >= WINDOW + tq
    assert S // SEL_BLOCK <= LANES and nk % LANES == 0
    nq = S // tq
    qw = hpg * dh
    kern = functools.partial(_nsa_kernel, tq=tq, tk=tk, n_sel=n_sel)
    return pl.pallas_call(
        kern,
        grid=(B, G, nq),
        in_specs=[
            pl.BlockSpec((tq, qw), lambda b, g, i: (b * nq + i, lay["q"] // qw + g)),
            pl.BlockSpec((tq, LANES), lambda b, g, i: (b * nq + i, lay["gates"] // LANES + g)),
            pl.BlockSpec((None, None, nk, dh), lambda b, g, i: (b, g, 0, 0)),
            pl.BlockSpec((None, None, nk, dh), lambda b, g, i: (b, g, 0, 0)),
            pl.BlockSpec((nk, LANES), lambda b, g, i: (0, 0)),
            pl.BlockSpec((S, dh), lambda b, g, i: (b, lay["ks"] // dh + g)),
            pl.BlockSpec((S, dh), lambda b, g, i: (b, lay["vs"] // dh + g)),
            pl.BlockSpec((S, dh), lambda b, g, i: (b, lay["kw"] // dh + g)),
            pl.BlockSpec((S, dh), lambda b, g, i: (b, lay["vw"] // dh + g)),
        ],
        out_specs=pl.BlockSpec((tq, qw), lambda b, g, i: (b * nq + i, g)),
        out_shape=jax.ShapeDtypeStruct((B * S, G * qw), BF16),
        scratch_shapes=[
            pltpu.VMEM((S, 2 * dh), BF16),
            pltpu.VMEM((S, 2 * dh), BF16),
            pltpu.VMEM((hpg, tq, 2 * dh), BF16),
            pltpu.VMEM((hpg, tq, LANES), F32),
            pltpu.VMEM((hpg, tq, 2 * dh), F32),
            pltpu.VMEM((hpg, tq, tk), F32),
            pltpu.VMEM((LANES, tq), F32),
            pltpu.VMEM((LANES, tq), F32),
        ],
        compiler_params=_params("parallel", "parallel", "arbitrary"),
        name="nsa",
    )(P, P, kcmp, vcmp, overlap, P, P, P, P)


def _mix_kernel(xp_ref, halo_ref, b_ref, gp_ref, gn_ref, pw_ref, ps_ref, wpu_ref, wnu_ref, o_ref, *, tm, S):
    halo_rows = halo_ref.shape[0]
    gd = POOL_GROUP_DIM
    t0 = (pl.program_id(0) * tm) % S
    x = xp_ref[...].astype(F32)
    halo = halo_ref[...].astype(F32) * (t0 > 0).astype(F32)
    xc = jnp.concatenate([halo, x], axis=0)
    tpos = t0 + lax.broadcasted_iota(jnp.int32, (tm, 1), 0)
    a_up = jnp.zeros((tm, o_ref.shape[1]), F32)
    for gi, win in enumerate(POOL_WINDOWS):
        cols = slice(gi * gd, (gi + 1) * gd)
        s = xc[:, cols]
        d = 1
        while d < win:
            s = s + pltpu.roll(s, d, 0)
            d *= 2
        cnt = jnp.minimum(tpos + 1, win).astype(F32)
        pooled = s[halo_rows:] / cnt - x[:, cols]
        y = _dot(pooled.astype(BF16), pw_ref[gi]) * ps_ref[:, cols]
        a_up = a_up + _dot(y.astype(BF16), wpu_ref[cols, :])
    b_up = _dot(b_ref[...], wnu_ref[...])
    merged = (jax.nn.sigmoid(gp_ref[...].astype(F32)) * a_up
              + jax.nn.sigmoid(gn_ref[...].astype(F32)) * b_up)
    o_ref[...] = merged.astype(o_ref.dtype)


def _mix(P, b_attn, pool_w, pool_scale, w_pool_up, w_nsa_up, lay, *, S, tm=256):
    T = P.shape[0]
    pw = pool_w.shape[0] * pool_w.shape[1]
    D = w_pool_up.shape[1]
    halo = max(POOL_WINDOWS)
    assert S % tm == 0 and tm % halo == 0 and lay["pool"] == 0
    kern = functools.partial(_mix_kernel, tm=tm, S=S)
    return pl.pallas_call(
        kern,
        grid=(T // tm,),
        in_specs=[
            pl.BlockSpec((tm, pw), lambda i: (i, 0)),
            pl.BlockSpec((halo, pw), lambda i: (jnp.maximum(i * (tm // halo) - 1, 0), 0)),
            pl.BlockSpec((tm, b_attn.shape[1]), lambda i: (i, 0)),
            pl.BlockSpec((tm, D), lambda i: (i, lay["g_pool"] // D)),
            pl.BlockSpec((tm, D), lambda i: (i, lay["g_nsa"] // D)),
            pl.BlockSpec(pool_w.shape, lambda i: (0, 0, 0)),
            pl.BlockSpec((1, pw), lambda i: (0, 0)),
            pl.BlockSpec(w_pool_up.shape, lambda i: (0, 0)),
            pl.BlockSpec(w_nsa_up.shape, lambda i: (0, 0)),
        ],
        out_specs=pl.BlockSpec((tm, D), lambda i: (i, 0)),
        out_shape=jax.ShapeDtypeStruct((T, D), BF16),
        compiler_params=_params("parallel"),
        name="mix",
    )(P, P, b_attn, P, P, pool_w, pool_scale.reshape(1, pw), w_pool_up, w_nsa_up)


def _outproj_kernel(h_ref, m_ref, w_ref, o_ref):
    o_ref[...] = h_ref[...] + _dot(m_ref[...], w_ref[...])


def _outproj(h, merged, w_out, *, tm=512):
    T, D = h.shape
    assert T % tm == 0
    return pl.pallas_call(
        _outproj_kernel,
        grid=(T // tm,),
        in_specs=[
            pl.BlockSpec((tm, D), lambda i: (i, 0)),
            pl.BlockSpec((tm, D), lambda i: (i, 0)),
            pl.BlockSpec(w_out.shape, lambda i: (0, 0)),
        ],
        out_specs=pl.BlockSpec((tm, D), lambda i: (i, 0)),
        out_shape=jax.ShapeDtypeStruct((T, D), F32),
        compiler_params=_params("parallel"),
        name="out_proj",
    )(h, merged, w_out)


def _prep_w_in(w_in, pool_width, d_model):
    G, hpg, dh = N_KV_GROUPS, HEADS_PER_GROUP, HEAD_DIM
    qw, kvw = N_HEADS * dh, G * dh
    o_q = pool_width
    o_kv = o_q + qw
    o_gate = o_kv + 6 * kvw
    o_gp = o_gate + N_NSA_BRANCHES * N_HEADS
    o_gn = o_gp + d_model
    assert w_in.shape[1] == o_gn + d_model
    gate_cols = w_in[:, o_gate:o_gp].reshape(-1, N_NSA_BRANCHES, G, hpg)
    gate_cols = jnp.transpose(gate_cols, (0, 2, 1, 3)).reshape(-1, G, N_NSA_BRANCHES * hpg)
    gate_cols = jnp.pad(gate_cols, ((0, 0), (0, 0), (0, LANES - N_NSA_BRANCHES * hpg))).reshape(-1, G * LANES)
    w = jnp.concatenate([
        w_in[:, :o_q],
        w_in[:, o_q:o_kv] * (LOG2E * dh ** -0.5),
        w_in[:, o_kv:o_gate],
        w_in[:, o_gp:o_gn],
        w_in[:, o_gn:],
        gate_cols,
    ], axis=1).astype(BF16)
    lay = {"pool": 0, "q": o_q, "kc": o_kv, "vc": o_kv + kvw, "ks": o_kv + 2 * kvw, "vs": o_kv + 3 * kvw,
           "kw": o_kv + 4 * kvw, "vw": o_kv + 5 * kvw, "g_pool": o_gate, "g_nsa": o_gate + d_model,
           "gates": o_gate + 2 * d_model}
    return w, lay


def _overlap_matrix(nk, n_cmp, n_blk):
    n = jnp.arange(nk)[:, None]
    m = jnp.arange(LANES)[None, :]
    cs = n * CMP_STRIDE
    ss = m * SEL_BLOCK
    ov = (cs <= ss + SEL_BLOCK - 1) & (cs + CMP_BLOCK - 1 >= ss) & (n < n_cmp) & (m < n_blk)
    return ov.astype(BF16)


def kernel(x, ffn1_norm, ffn1_w_gate, ffn1_w_up, ffn1_w_down, mix_norm, w_in, pool_w, pool_scale,
           cmp_pe_k, cmp_pe_v, cmp_k_w1, cmp_k_w2, cmp_v_w1, cmp_v_w2, w_pool_up, w_nsa_up, w_out,
           ffn2_norm, ffn2_w_gate, ffn2_w_up, ffn2_w_down, final_norm):
    B, S, D = x.shape
    depth = ffn1_norm.shape[0]
    G, dh = N_KV_GROUPS, HEAD_DIM
    T = B * S
    pool_width = pool_w.shape[1] * pool_w.shape[2]
    n_chunk = S // CMP_STRIDE
    n_cmp = n_chunk - CMP_BLOCK // CMP_STRIDE + 1
    n_blk = S // SEL_BLOCK
    cw = CMP_STRIDE * dh

    h = x.reshape(T, D)
    for l in range(depth):
        last = l == depth - 1
        h = _ffn(h, ffn1_norm[l], ffn1_w_gate[l].astype(BF16), ffn1_w_up[l].astype(BF16),
                 ffn1_w_down[l].astype(BF16))

        w_p, lay = _prep_w_in(w_in[l], pool_width, D)
        P = _proj(h, mix_norm[l], w_p)

        def group_major(name):
            cols = P[:, lay[name]:lay[name] + G * dh]
            return jnp.transpose(cols.reshape(B, S, G, dh), (0, 2, 1, 3))

        ckv = jnp.stack([group_major("kc"), group_major("vc")]).reshape(2, B, G, n_chunk, cw)
        w1 = jnp.stack([cmp_k_w1[l], cmp_v_w1[l]]).astype(BF16)
        w2 = jnp.stack([cmp_k_w2[l], cmp_v_w2[l]]).astype(BF16)
        pe = jnp.stack([cmp_pe_k[l], cmp_pe_v[l]]).reshape(2, 1, CMP_BLOCK * dh).astype(BF16)
        cmp_kv = _compress(ckv, w1, w2, pe)

        overlap = _overlap_matrix(n_chunk, n_cmp, n_blk)

        b_attn = _nsa(P, cmp_kv[0], cmp_kv[1], overlap, lay, B=B, S=S)
        merged = _mix(P, b_attn, pool_w[l].astype(BF16), pool_scale[l], w_pool_up[l].astype(BF16),
                      w_nsa_up[l].astype(BF16), lay, S=S)
        h = _outproj(h, merged, w_out[l].astype(BF16))
        h = _ffn(h, ffn2_norm[l], ffn2_w_gate[l].astype(BF16), ffn2_w_up[l].astype(BF16),
                 ffn2_w_down[l].astype(BF16), final_norm if last else None)
    if depth == 0:
        raise ValueError("depth must be positive")
    return h.reshape(B, S, D)
```

```python
import functools

import jax
import jax.numpy as jnp
from jax import lax
from jax.experimental import pallas as pl
from jax.experimental.pallas import tpu as pltpu

F32 = jnp.float32
BF16 = jnp.bfloat16

POOL_WINDOWS = (2, 4, 8, 16)
POOL_GROUP_DIM = 256
N_HEADS = 16
N_KV_GROUPS = 4
HEADS_PER_GROUP = N_HEADS // N_KV_GROUPS
HEAD_DIM = 128
CMP_BLOCK = 32
CMP_STRIDE = 16
SEL_BLOCK = 64
SEL_TOPK = 16
WINDOW = 512
N_NSA_BRANCHES = 3
FORCE_BONUS = 1000.0
RMS_EPS = 1e-6

LOG2E = 1.4426950408889634
LANES = 128
MASK_BIAS = -1e30
VMEM_LIMIT = 56 * 1024 * 1024


def _dot(a, b):
    return jnp.dot(a, b, preferred_element_type=F32)


def _dot_nt(a, b):
    return lax.dot_general(a, b, (((1,), (1,)), ((), ())), preferred_element_type=F32)


def _rms(x, g):
    ms = jnp.mean(x * x, axis=-1, keepdims=True)
    return x * lax.rsqrt(ms + RMS_EPS) * g


def _params(*sem):
    return pltpu.CompilerParams(dimension_semantics=sem, vmem_limit_bytes=VMEM_LIMIT)


def _ffn_kernel(*refs, final):
    if final:
        x_ref, g_ref, wg_ref, wu_ref, wd_ref, fg_ref, o_ref, xn_ref, acc_ref = refs
    else:
        x_ref, g_ref, wg_ref, wu_ref, wd_ref, o_ref, xn_ref, acc_ref = refs
    j = pl.program_id(1)

    @pl.when(j == 0)
    def _():
        xn_ref[...] = _rms(x_ref[...], g_ref[...]).astype(BF16)
        acc_ref[...] = jnp.zeros_like(acc_ref)

    xn = xn_ref[...]
    gate = _dot(xn, wg_ref[...])
    up = _dot(xn, wu_ref[...])
    act = (gate * jax.nn.sigmoid(gate) * up).astype(BF16)
    acc_ref[...] += _dot(act, wd_ref[...])

    @pl.when(j == pl.num_programs(1) - 1)
    def _():
        h = x_ref[...] + 0.5 * acc_ref[...]
        if final:
            h = _rms(h, fg_ref[...])
        o_ref[...] = h


def _ffn(x, g, wg, wu, wd, final_g=None, *, tm=512, tf=512):
    T, D = x.shape
    F = wg.shape[1]
    assert T % tm == 0 and F % tf == 0
    final = final_g is not None
    in_specs = [
        pl.BlockSpec((tm, D), lambda i, j: (i, 0)),
        pl.BlockSpec((1, D), lambda i, j: (0, 0)),
        pl.BlockSpec((D, tf), lambda i, j: (0, j)),
        pl.BlockSpec((D, tf), lambda i, j: (0, j)),
        pl.BlockSpec((tf, D), lambda i, j: (j, 0)),
    ]
    args = [x, g.reshape(1, D), wg, wu, wd]
    if final:
        in_specs.append(pl.BlockSpec((1, D), lambda i, j: (0, 0)))
        args.append(final_g.reshape(1, D))
    return pl.pallas_call(
        functools.partial(_ffn_kernel, final=final),
        grid=(T // tm, F // tf),
        in_specs=in_specs,
        out_specs=pl.BlockSpec((tm, D), lambda i, j: (i, 0)),
        out_shape=jax.ShapeDtypeStruct((T, D), F32),
        scratch_shapes=[pltpu.VMEM((tm, D), BF16), pltpu.VMEM((tm, D), F32)],
        compiler_params=_params("parallel", "arbitrary"),
        name="ffn_final" if final else "ffn",
    )(*args)


def _proj_kernel(x_ref, g_ref, w_ref, o_ref, xn_ref):
    @pl.when(pl.program_id(1) == 0)
    def _():
        xn_ref[...] = _rms(x_ref[...], g_ref[...]).astype(BF16)

    o_ref[...] = _dot(xn_ref[...], w_ref[...]).astype(o_ref.dtype)


def _proj(x, g, w, *, tm=1024, tn=1536):
    T, D = x.shape
    N = w.shape[1]
    tm = min(tm, T)
    assert T % tm == 0 and N % tn == 0
    return pl.pallas_call(
        _proj_kernel,
        grid=(T // tm, N // tn),
        in_specs=[
            pl.BlockSpec((tm, D), lambda i, j: (i, 0)),
            pl.BlockSpec((1, D), lambda i, j: (0, 0)),
            pl.BlockSpec((D, tn), lambda i, j: (0, j)),
        ],
        out_specs=pl.BlockSpec((tm, tn), lambda i, j: (i, j)),
        out_shape=jax.ShapeDtypeStruct((T, N), BF16),
        scratch_shapes=[pltpu.VMEM((tm, D), BF16)],
        compiler_params=_params("parallel", "arbitrary"),
        name="in_proj",
    )(x, g.reshape(1, D), w)


def _cmp_kernel(c_ref, w1_ref, w2_ref, pe_ref, o_ref):
    c = c_ref[...]
    nch, half = c.shape
    a = _dot(c, w1_ref[:half, :])
    b = _dot(c, w1_ref[half:, :])
    pe = jnp.broadcast_to(pe_ref[...], (8, 2 * half))
    pe_term = _dot(pe, w1_ref[...])[0:1]
    pre = a + pltpu.roll(b, nch - 1, 0) + pe_term
    hid = (pre * jax.nn.sigmoid(pre)).astype(BF16)
    o_ref[...] = _dot(hid, w2_ref[...]).astype(o_ref.dtype)


def _compress(ckv, w1, w2, pe):
    _, B, G, nch, cw = ckv.shape
    hid = w1.shape[2]
    return pl.pallas_call(
        _cmp_kernel,
        grid=(2, B, G),
        in_specs=[
            pl.BlockSpec((None, None, None, nch, cw), lambda s, b, g: (s, b, g, 0, 0)),
            pl.BlockSpec((None, 2 * cw, hid), lambda s, b, g: (s, 0, 0)),
            pl.BlockSpec((None, hid, HEAD_DIM), lambda s, b, g: (s, 0, 0)),
            pl.BlockSpec((None, 1, 2 * cw), lambda s, b, g: (s, 0, 0)),
        ],
        out_specs=pl.BlockSpec((None, None, None, nch, HEAD_DIM), lambda s, b, g: (s, b, g, 0, 0)),
        out_shape=jax.ShapeDtypeStruct((2, B, G, nch, HEAD_DIM), BF16),
        compiler_params=_params("parallel", "parallel", "parallel"),
        name="compress",
    )(ckv, w1, w2, pe)


def _nsa_kernel(q_ref, gl_ref, kc_ref, vc_ref, ov_ref, ks_ref, vs_ref, kw_ref, vw_ref, o_ref,
                ka_ref, ve_ref, qa_ref, m_ref, acc_ref, s_ref, imp_ref, bt_ref, *, tq, tk, n_sel):
    hpg, dh = HEADS_PER_GROUP, HEAD_DIM
    s0 = pl.program_id(2) * tq
    t_col = s0 + lax.broadcasted_iota(jnp.int32, (tq, 1), 0)

    @pl.when(pl.program_id(2) == 0)
    def _():
        seq = ks_ref.shape[0]
        for c in range(seq // tk):
            r = slice(c * tk, (c + 1) * tk)
            kblk = lax.shift_right_logical(c * tk + lax.broadcasted_iota(jnp.int32, (tk, LANES), 0),
                                           SEL_BLOCK.bit_length() - 1)
            lane = lax.broadcasted_iota(jnp.int32, (tk, LANES), 1)
            ka_ref[r, 0:dh] = ks_ref[r, :]
            ka_ref[r, dh:2 * dh] = (kblk == lane).astype(BF16)
            ve_ref[r, 0:dh] = vs_ref[r, :]
            ve_ref[r, dh:2 * dh] = (lane == 0).astype(BF16)

    for h in range(hpg):
        qa_ref[h, :, 0:dh] = q_ref[:, h * dh:(h + 1) * dh]

    nk = kc_ref.shape[0]
    n_idx = lax.broadcasted_iota(jnp.int32, (tq, nk), 1)
    mask_c = (n_idx * CMP_STRIDE + (CMP_BLOCK - 1)) <= t_col
    o_c = []
    p_sum = None
    for h in range(hpg):
        s_c = jnp.where(mask_c, _dot_nt(qa_ref[h, :, 0:dh], kc_ref[...]), -jnp.inf)
        m_c = jnp.max(s_c, axis=-1, keepdims=True)
        m_c = jnp.where(m_c == -jnp.inf, 0.0, m_c)
        p_c = jnp.exp2(s_c - m_c)
        p_c = p_c * (1.0 / jnp.maximum(jnp.sum(p_c, axis=-1, keepdims=True), 1e-30))
        o_c.append(_dot(p_c.astype(BF16), vc_ref[...]))
        p_sum = p_c if p_sum is None else p_sum + p_c

    p_hi = p_sum.astype(BF16)
    p_lo = (p_sum - p_hi.astype(F32)).astype(BF16)
    imp_t = (_dot(p_hi, ov_ref[...]) + _dot(p_lo, ov_ref[...])).T

    imp_ref[...] = imp_t
    bt_ref[...] = jnp.full_like(bt_ref, MASK_BIAS)

    def select_rows(nrows):
        blk = lax.broadcasted_iota(jnp.int32, (nrows, tq), 0)
        t_row = s0 + lax.broadcasted_iota(jnp.int32, (nrows, tq), 1)
        cur = lax.shift_right_logical(t_row, SEL_BLOCK.bit_length() - 1)
        forced = (blk == 0) | (blk == cur) | (blk == cur - 1)
        cand = jnp.where(forced | (blk * SEL_BLOCK > t_row), -FORCE_BONUS, imp_ref[0:nrows, :])
        ridx = blk.astype(F32)
        sc = cand
        for _ in range(n_sel - 3):
            mx = jnp.max(sc, axis=0, keepdims=True)
            idx = jnp.min(jnp.where(sc == mx, ridx, float(LANES)), axis=0, keepdims=True)
            sc = jnp.where(ridx == idx, -jnp.inf, sc)
        sel = forced | ((sc == -jnp.inf) & (cand > -1.0))
        bt_ref[0:nrows, :] = jnp.where(sel, 0.0, MASK_BIAS)

    n_live = (s0 + tq) // SEL_BLOCK
    lo, mid = LANES // 4, LANES // 2
    pl.when(n_live <= lo)(lambda: select_rows(lo))
    pl.when((n_live > lo) & (n_live <= mid))(lambda: select_rows(mid))
    pl.when(n_live > mid)(lambda: select_rows(LANES))
    bias = bt_ref[...].T.astype(BF16)
    for h in range(hpg):
        qa_ref[h, :, dh:2 * dh] = bias

    m_ref[...] = jnp.full_like(m_ref, MASK_BIAS)
    acc_ref[...] = jnp.zeros_like(acc_ref)

    def scores(j, h):
        return _dot_nt(qa_ref[h], ka_ref[pl.ds(pl.multiple_of(j * tk, tk), tk), :])

    def consume(s, j, h):
        m_prev = m_ref[h]
        m_new = jnp.maximum(m_prev, jnp.max(s, axis=-1, keepdims=True))
        alpha = jnp.exp2(m_prev - m_new)
        p = jnp.exp2(s - pltpu.repeat(m_new, tk // LANES, axis=1))
        pv = _dot(p.astype(BF16), ve_ref[pl.ds(pl.multiple_of(j * tk, tk), tk), :])
        acc_ref[h] = acc_ref[h] * pltpu.repeat(alpha, 2, axis=1) + pv
        m_ref[h] = m_new

    for h in range(hpg):
        s_ref[h] = scores(0, h)
    j_last = s0 // tk

    def body(j, carry):
        for h in range(hpg):
            s_next = scores(j + 1, h)
            consume(s_ref[h], j, h)
            s_ref[h] = s_next
        return carry

    lax.fori_loop(0, j_last, body, 0)
    causal = j_last * tk + lax.broadcasted_iota(jnp.int32, (tq, tk), 1) <= t_col
    for h in range(hpg):
        consume(jnp.where(causal, s_ref[h], MASK_BIAS), j_last, h)

    wk = WINDOW + tq
    start = pl.multiple_of(jnp.maximum(s0 - WINDOW, 0), tq)
    kpos = start + lax.broadcasted_iota(jnp.int32, (tq, wk), 1)
    mask_w = (kpos <= t_col) & (kpos > t_col - WINDOW)
    gates = jax.nn.sigmoid(gl_ref[...].astype(F32))
    for h in range(hpg):
        s_w = jnp.where(mask_w, _dot_nt(qa_ref[h, :, 0:dh], kw_ref[pl.ds(start, wk), :]), -jnp.inf)
        p_w = jnp.exp2(s_w - jnp.max(s_w, axis=-1, keepdims=True))
        r_w = 1.0 / jnp.sum(p_w, axis=-1, keepdims=True)
        o_w = _dot(p_w.astype(BF16), vw_ref[pl.ds(start, wk), :]) * r_w
        acc = acc_ref[h]
        o_s = acc[:, 0:dh] * (1.0 / acc[:, dh:dh + 1])
        out = (gates[:, h:h + 1] * o_c[h]
               + gates[:, hpg + h:hpg + h + 1] * o_s
               + gates[:, 2 * hpg + h:2 * hpg + h + 1] * o_w)
        o_ref[:, h * dh:(h + 1) * dh] = out.astype(o_ref.dtype)


def _nsa(P, kcmp, vcmp, overlap, lay, *, B, S, tq=512, tk=512):
    G, dh, hpg = N_KV_GROUPS, HEAD_DIM, HEADS_PER_GROUP
    tq = min(tq, S)
    tk = min(tk, S)
    nk = kcmp.shape[2]
    n_sel = min(SEL_TOPK, S // SEL_BLOCK)
    assert S % tq == 0 and tk % tq == 0 and S % tk == 0 and S >= WINDOW + tq
    assert S // SEL_BLOCK <= LANES and nk % LANES == 0
    nq = S // tq
    qw = hpg * dh
    kern = functools.partial(_nsa_kernel, tq=tq, tk=tk, n_sel=n_sel)
    return pl.pallas_call(
        kern,
        grid=(B, G, nq),
        in_specs=[
            pl.BlockSpec((tq, qw), lambda b, g, i: (b * nq + i, lay["q"] // qw + g)),
            pl.BlockSpec((tq, LANES), lambda b, g, i: (b * nq + i, lay["gates"] // LANES + g)),
            pl.BlockSpec((None, None, nk, dh), lambda b, g, i: (b, g, 0, 0)),
            pl.BlockSpec((None, None, nk, dh), lambda b, g, i: (b, g, 0, 0)),
            pl.BlockSpec((nk, LANES), lambda b, g, i: (0, 0)),
            pl.BlockSpec((S, dh), lambda b, g, i: (b, lay["ks"] // dh + g)),
            pl.BlockSpec((S, dh), lambda b, g, i: (b, lay["vs"] // dh + g)),
            pl.BlockSpec((S, dh), lambda b, g, i: (b, lay["kw"] // dh + g)),
            pl.BlockSpec((S, dh), lambda b, g, i: (b, lay["vw"] // dh + g)),
        ],
        out_specs=pl.BlockSpec((tq, qw), lambda b, g, i: (b * nq + i, g)),
        out_shape=jax.ShapeDtypeStruct((B * S, G * qw), BF16),
        scratch_shapes=[
            pltpu.VMEM((S, 2 * dh), BF16),
            pltpu.VMEM((S, 2 * dh), BF16),
            pltpu.VMEM((hpg, tq, 2 * dh), BF16),
            pltpu.VMEM((hpg, tq, LANES), F32),
            pltpu.VMEM((hpg, tq, 2 * dh), F32),
            pltpu.VMEM((hpg, tq, tk), F32),
            pltpu.VMEM((LANES, tq), F32),
            pltpu.VMEM((LANES, tq), F32),
        ],
        compiler_params=_params("parallel", "parallel", "arbitrary"),
        name="nsa",
    )(P, P, kcmp, vcmp, overlap, P, P, P, P)


def _mix_kernel(xp_ref, halo_ref, b_ref, gp_ref, gn_ref, pw_ref, ps_ref, wpu_ref, wnu_ref, o_ref, *, tm, S):
    halo_rows = halo_ref.shape[0]
    gd = POOL_GROUP_DIM
    t0 = (pl.program_id(0) * tm) % S
    x = xp_ref[...].astype(F32)
    halo = halo_ref[...].astype(F32) * (t0 > 0).astype(F32)
    xc = jnp.concatenate([halo, x], axis=0)
    tpos = t0 + lax.broadcasted_iota(jnp.int32, (tm, 1), 0)
    a_up = jnp.zeros((tm, o_ref.shape[1]), F32)
    for gi, win in enumerate(POOL_WINDOWS):
        cols = slice(gi * gd, (gi + 1) * gd)
        s = xc[:, cols]
        d = 1
        while d < win:
            s = s + pltpu.roll(s, d, 0)
            d *= 2
        cnt = jnp.minimum(tpos + 1, win).astype(F32)
        pooled = s[halo_rows:] / cnt - x[:, cols]
        y = _dot(pooled.astype(BF16), pw_ref[gi]) * ps_ref[:, cols]
        a_up = a_up + _dot(y.astype(BF16), wpu_ref[cols, :])
    b_up = _dot(b_ref[...], wnu_ref[...])
    merged = (jax.nn.sigmoid(gp_ref[...].astype(F32)) * a_up
              + jax.nn.sigmoid(gn_ref[...].astype(F32)) * b_up)
    o_ref[...] = merged.astype(o_ref.dtype)


def _mix(P, b_attn, pool_w, pool_scale, w_pool_up, w_nsa_up, lay, *, S, tm=256):
    T = P.shape[0]
    pw = pool_w.shape[0] * pool_w.shape[1]
    D = w_pool_up.shape[1]
    halo = max(POOL_WINDOWS)
    assert S % tm == 0 and tm % halo == 0 and lay["pool"] == 0
    kern = functools.partial(_mix_kernel, tm=tm, S=S)
    return pl.pallas_call(
        kern,
        grid=(T // tm,),
        in_specs=[
            pl.BlockSpec((tm, pw), lambda i: (i, 0)),
            pl.BlockSpec((halo, pw), lambda i: (jnp.maximum(i * (tm // halo) - 1, 0), 0)),
            pl.BlockSpec((tm, b_attn.shape[1]), lambda i: (i, 0)),
            pl.BlockSpec((tm, D), lambda i: (i, lay["g_pool"] // D)),
            pl.BlockSpec((tm, D), lambda i: (i, lay["g_nsa"] // D)),
            pl.BlockSpec(pool_w.shape, lambda i: (0, 0, 0)),
            pl.BlockSpec((1, pw), lambda i: (0, 0)),
            pl.BlockSpec(w_pool_up.shape, lambda i: (0, 0)),
            pl.BlockSpec(w_nsa_up.shape, lambda i: (0, 0)),
        ],
        out_specs=pl.BlockSpec((tm, D), lambda i: (i, 0)),
        out_shape=jax.ShapeDtypeStruct((T, D), BF16),
        compiler_params=_params("parallel"),
        name="mix",
    )(P, P, b_attn, P, P, pool_w, pool_scale.reshape(1, pw), w_pool_up, w_nsa_up)


def _outproj_kernel(h_ref, m_ref, w_ref, o_ref):
    o_ref[...] = h_ref[...] + _dot(m_ref[...], w_ref[...])


def _outproj(h, merged, w_out, *, tm=512):
    T, D = h.shape
    assert T % tm == 0
    return pl.pallas_call(
        _outproj_kernel,
        grid=(T // tm,),
        in_specs=[
            pl.BlockSpec((tm, D), lambda i: (i, 0)),
            pl.BlockSpec((tm, D), lambda i: (i, 0)),
            pl.BlockSpec(w_out.shape, lambda i: (0, 0)),
        ],
        out_specs=pl.BlockSpec((tm, D), lambda i: (i, 0)),
        out_shape=jax.ShapeDtypeStruct((T, D), F32),
        compiler_params=_params("parallel"),
        name="out_proj",
    )(h, merged, w_out)


def _prep_w_in(w_in, pool_width, d_model):
    G, hpg, dh = N_KV_GROUPS, HEADS_PER_GROUP, HEAD_DIM
    qw, kvw = N_HEADS * dh, G * dh
    o_q = pool_width
    o_kv = o_q + qw
    o_gate = o_kv + 6 * kvw
    o_gp = o_gate + N_NSA_BRANCHES * N_HEADS
    o_gn = o_gp + d_model
    assert w_in.shape[1] == o_gn + d_model
    gate_cols = w_in[:, o_gate:o_gp].reshape(-1, N_NSA_BRANCHES, G, hpg)
    gate_cols = jnp.transpose(gate_cols, (0, 2, 1, 3)).reshape(-1, G, N_NSA_BRANCHES * hpg)
    gate_cols = jnp.pad(gate_cols, ((0, 0), (0, 0), (0, LANES - N_NSA_BRANCHES * hpg))).reshape(-1, G * LANES)
    w = jnp.concatenate([
        w_in[:, :o_q],
        w_in[:, o_q:o_kv] * (LOG2E * dh ** -0.5),
        w_in[:, o_kv:o_gate],
        w_in[:, o_gp:o_gn],
        w_in[:, o_gn:],
        gate_cols,
    ], axis=1).astype(BF16)
    lay = {"pool": 0, "q": o_q, "kc": o_kv, "vc": o_kv + kvw, "ks": o_kv + 2 * kvw, "vs": o_kv + 3 * kvw,
           "kw": o_kv + 4 * kvw, "vw": o_kv + 5 * kvw, "g_pool": o_gate, "g_nsa": o_gate + d_model,
           "gates": o_gate + 2 * d_model}
    return w, lay


def _overlap_matrix(nk, n_cmp, n_blk):
    n = jnp.arange(nk)[:, None]
    m = jnp.arange(LANES)[None, :]
    cs = n * CMP_STRIDE
    ss = m * SEL_BLOCK
    ov = (cs <= ss + SEL_BLOCK - 1) & (cs + CMP_BLOCK - 1 >= ss) & (n < n_cmp) & (m < n_blk)
    return ov.astype(BF16)


def kernel(x, ffn1_norm, ffn1_w_gate, ffn1_w_up, ffn1_w_down, mix_norm, w_in, pool_w, pool_scale,
           cmp_pe_k, cmp_pe_v, cmp_k_w1, cmp_k_w2, cmp_v_w1, cmp_v_w2, w_pool_up, w_nsa_up, w_out,
           ffn2_norm, ffn2_w_gate, ffn2_w_up, ffn2_w_down, final_norm):
    B, S, D = x.shape
    depth = ffn1_norm.shape[0]
    G, dh = N_KV_GROUPS, HEAD_DIM
    T = B * S
    pool_width = pool_w.shape[1] * pool_w.shape[2]
    n_chunk = S // CMP_STRIDE
    n_cmp = n_chunk - CMP_BLOCK // CMP_STRIDE + 1
    n_blk = S // SEL_BLOCK
    cw = CMP_STRIDE * dh

    h = x.reshape(T, D)
    for l in range(depth):
        last = l == depth - 1
        h = _ffn(h, ffn1_norm[l], ffn1_w_gate[l].astype(BF16), ffn1_w_up[l].astype(BF16),
                 ffn1_w_down[l].astype(BF16))

        w_p, lay = _prep_w_in(w_in[l], pool_width, D)
        P = _proj(h, mix_norm[l], w_p)

        def group_major(name):
            cols = P[:, lay[name]:lay[name] + G * dh]
            return jnp.transpose(cols.reshape(B, S, G, dh), (0, 2, 1, 3))

        ckv = jnp.stack([group_major("kc"), group_major("vc")]).reshape(2, B, G, n_chunk, cw)
        w1 = jnp.stack([cmp_k_w1[l], cmp_v_w1[l]]).astype(BF16)
        w2 = jnp.stack([cmp_k_w2[l], cmp_v_w2[l]]).astype(BF16)
        pe = jnp.stack([cmp_pe_k[l], cmp_pe_v[l]]).reshape(2, 1, CMP_BLOCK * dh).astype(BF16)
        cmp_kv = _compress(ckv, w1, w2, pe)

        overlap = _overlap_matrix(n_chunk, n_cmp, n_blk)

        b_attn = _nsa(P, cmp_kv[0], cmp_kv[1], overlap, lay, B=B, S=S)
        merged = _mix(P, b_attn, pool_w[l].astype(BF16), pool_scale[l], w_pool_up[l].astype(BF16),
                      w_nsa_up[l].astype(BF16), lay, S=S)
        h = _outproj(h, merged, w_out[l].astype(BF16))
        h = _ffn(h, ffn2_norm[l], ffn2_w_gate[l].astype(BF16), ffn2_w_up[l].astype(BF16),
                 ffn2_w_down[l].astype(BF16), final_norm if last else None)
    if depth == 0:
        raise ValueError("depth must be positive")
    return h.reshape(B, S, D)
```

```python
import functools

import jax
import jax.numpy as jnp
from jax import lax
from jax.experimental import pallas as pl
from jax.experimental.pallas import tpu as pltpu

F32 = jnp.float32
BF16 = jnp.bfloat16

POOL_WINDOWS = (2, 4, 8, 16)
POOL_GROUP_DIM = 256
N_HEADS = 16
N_KV_GROUPS = 4
HEADS_PER_GROUP = N_HEADS // N_KV_GROUPS
HEAD_DIM = 128
CMP_BLOCK = 32
CMP_STRIDE = 16
SEL_BLOCK = 64
SEL_TOPK = 16
WINDOW = 512
N_NSA_BRANCHES = 3
FORCE_BONUS = 1000.0
RMS_EPS = 1e-6

LOG2E = 1.4426950408889634
LANES = 128
MASK_BIAS = -1e30
VMEM_LIMIT = 56 * 1024 * 1024


def _dot(a, b):
    return jnp.dot(a, b, preferred_element_type=F32)


def _dot_nt(a, b):
    return lax.dot_general(a, b, (((1,), (1,)), ((), ())), preferred_element_type=F32)


def _rms(x, g):
    ms = jnp.mean(x * x, axis=-1, keepdims=True)
    return x * lax.rsqrt(ms + RMS_EPS) * g


def _params(*sem):
    return pltpu.CompilerParams(dimension_semantics=sem, vmem_limit_bytes=VMEM_LIMIT)


def _ffn_kernel(*refs, final):
    if final:
        x_ref, g_ref, wg_ref, wu_ref, wd_ref, fg_ref, o_ref, xn_ref, acc_ref = refs
    else:
        x_ref, g_ref, wg_ref, wu_ref, wd_ref, o_ref, xn_ref, acc_ref = refs
    j = pl.program_id(1)

    @pl.when(j == 0)
    def _():
        xn_ref[...] = _rms(x_ref[...], g_ref[...]).astype(BF16)
        acc_ref[...] = jnp.zeros_like(acc_ref)

    xn = xn_ref[...]
    gate = _dot(xn, wg_ref[...])
    up = _dot(xn, wu_ref[...])
    act = (gate * jax.nn.sigmoid(gate) * up).astype(BF16)
    acc_ref[...] += _dot(act, wd_ref[...])

    @pl.when(j == pl.num_programs(1) - 1)
    def _():
        h = x_ref[...] + 0.5 * acc_ref[...]
        if final:
            h = _rms(h, fg_ref[...])
        o_ref[...] = h


def _ffn(x, g, wg, wu, wd, final_g=None, *, tm=512, tf=512):
    T, D = x.shape
    F = wg.shape[1]
    assert T % tm == 0 and F % tf == 0
    final = final_g is not None
    in_specs = [
        pl.BlockSpec((tm, D), lambda i, j: (i, 0)),
        pl.BlockSpec((1, D), lambda i, j: (0, 0)),
        pl.BlockSpec((D, tf), lambda i, j: (0, j)),
        pl.BlockSpec((D, tf), lambda i, j: (0, j)),
        pl.BlockSpec((tf, D), lambda i, j: (j, 0)),
    ]
    args = [x, g.reshape(1, D), wg, wu, wd]
    if final:
        in_specs.append(pl.BlockSpec((1, D), lambda i, j: (0, 0)))
        args.append(final_g.reshape(1, D))
    return pl.pallas_call(
        functools.partial(_ffn_kernel, final=final),
        grid=(T // tm, F // tf),
        in_specs=in_specs,
        out_specs=pl.BlockSpec((tm, D), lambda i, j: (i, 0)),
        out_shape=jax.ShapeDtypeStruct((T, D), F32),
        scratch_shapes=[pltpu.VMEM((tm, D), BF16), pltpu.VMEM((tm, D), F32)],
        compiler_params=_params("parallel", "arbitrary"),
        name="ffn_final" if final else "ffn",
    )(*args)


def _proj_kernel(x_ref, g_ref, w_ref, o_ref, xn_ref):
    @pl.when(pl.program_id(1) == 0)
    def _():
        xn_ref[...] = _rms(x_ref[...], g_ref[...]).astype(BF16)

    o_ref[...] = _dot(xn_ref[...], w_ref[...]).astype(o_ref.dtype)


def _proj(x, g, w, *, tm=1024, tn=1536):
    T, D = x.shape
    N = w.shape[1]
    tm = min(tm, T)
    assert T % tm == 0 and N % tn == 0
    return pl.pallas_call(
        _proj_kernel,
        grid=(T // tm, N // tn),
        in_specs=[
            pl.BlockSpec((tm, D), lambda i, j: (i, 0)),
            pl.BlockSpec((1, D), lambda i, j: (0, 0)),
            pl.BlockSpec((D, tn), lambda i, j: (0, j)),
        ],
        out_specs=pl.BlockSpec((tm, tn), lambda i, j: (i, j)),
        out_shape=jax.ShapeDtypeStruct((T, N), BF16),
        scratch_shapes=[pltpu.VMEM((tm, D), BF16)],
        compiler_params=_params("parallel", "arbitrary"),
        name="in_proj",
    )(x, g.reshape(1, D), w)


def _cmp_kernel(t_ref, w1_ref, w2_ref, pe_ref, o_ref, xf_ref, c_ref):
    seq, dh = t_ref.shape
    nch, half = c_ref.shape
    xf_ref[...] = t_ref[...].astype(F32)
    for j in range(CMP_STRIDE):
        c_ref[:, j * dh:(j + 1) * dh] = xf_ref[pl.ds(j, nch, stride=CMP_STRIDE), :].astype(BF16)
    c = c_ref[...]
    a = _dot(c, w1_ref[:half, :])
    b = _dot(c, w1_ref[half:, :])
    pe = jnp.broadcast_to(pe_ref[...], (8, 2 * half))
    pe_term = _dot(pe, w1_ref[...])[0:1]
    pre = a + pltpu.roll(b, nch - 1, 0) + pe_term
    hid = (pre * jax.nn.sigmoid(pre)).astype(BF16)
    o_ref[...] = _dot(hid, w2_ref[...]).astype(o_ref.dtype)


def _compress(P, w1, w2, pe, lay, *, B, S):
    G, dh = N_KV_GROUPS, HEAD_DIM
    nch = S // CMP_STRIDE
    cw = CMP_STRIDE * dh
    hid = w1.shape[2]
    kc_blk, kv_stride = lay["kc"] // dh, (lay["vc"] - lay["kc"]) // dh
    return pl.pallas_call(
        _cmp_kernel,
        grid=(2, B, G),
        in_specs=[
            pl.BlockSpec((S, dh), lambda s, b, g: (b, kc_blk + s * kv_stride + g)),
            pl.BlockSpec((None, 2 * cw, hid), lambda s, b, g: (s, 0, 0)),
            pl.BlockSpec((None, hid, dh), lambda s, b, g: (s, 0, 0)),
            pl.BlockSpec((None, 1, 2 * cw), lambda s, b, g: (s, 0, 0)),
        ],
        out_specs=pl.BlockSpec((None, None, None, nch, dh), lambda s, b, g: (s, b, g, 0, 0)),
        out_shape=jax.ShapeDtypeStruct((2, B, G, nch, dh), BF16),
        scratch_shapes=[pltpu.VMEM((S, dh), F32), pltpu.VMEM((nch, cw), BF16)],
        compiler_params=_params("parallel", "parallel", "parallel"),
        name="compress",
    )(P, w1, w2, pe)


def _nsa_kernel(q_ref, gl_ref, kc_ref, vc_ref, ov_ref, ks_ref, vs_ref, kw_ref, vw_ref, o_ref,
                ka_ref, ve_ref, qa_ref, m_ref, acc_ref, s_ref, imp_ref, bt_ref, *, tq, tk, n_sel):
    hpg, dh = HEADS_PER_GROUP, HEAD_DIM
    s0 = pl.program_id(2) * tq
    t_col = s0 + lax.broadcasted_iota(jnp.int32, (tq, 1), 0)

    @pl.when(pl.program_id(2) == 0)
    def _():
        seq = ks_ref.shape[0]
        for c in range(seq // tk):
            r = slice(c * tk, (c + 1) * tk)
            kblk = lax.shift_right_logical(c * tk + lax.broadcasted_iota(jnp.int32, (tk, LANES), 0),
                                           SEL_BLOCK.bit_length() - 1)
            lane = lax.broadcasted_iota(jnp.int32, (tk, LANES), 1)
            ka_ref[r, 0:dh] = ks_ref[r, :]
            ka_ref[r, dh:2 * dh] = (kblk == lane).astype(BF16)
            ve_ref[r, 0:dh] = vs_ref[r, :]
            ve_ref[r, dh:2 * dh] = (lane == 0).astype(BF16)

    for h in range(hpg):
        qa_ref[h, :, 0:dh] = q_ref[:, h * dh:(h + 1) * dh]

    nk = kc_ref.shape[0]
    n_idx = lax.broadcasted_iota(jnp.int32, (tq, nk), 1)
    mask_c = (n_idx * CMP_STRIDE + (CMP_BLOCK - 1)) <= t_col
    o_c = []
    p_sum = None
    for h in range(hpg):
        s_c = jnp.where(mask_c, _dot_nt(qa_ref[h, :, 0:dh], kc_ref[...]), -jnp.inf)
        m_c = jnp.max(s_c, axis=-1, keepdims=True)
        m_c = jnp.where(m_c == -jnp.inf, 0.0, m_c)
        p_c = jnp.exp2(s_c - m_c)
        p_c = p_c * (1.0 / jnp.maximum(jnp.sum(p_c, axis=-1, keepdims=True), 1e-30))
        o_c.append(_dot(p_c.astype(BF16), vc_ref[...]))
        p_sum = p_c if p_sum is None else p_sum + p_c

    p_hi = p_sum.astype(BF16)
    p_lo = (p_sum - p_hi.astype(F32)).astype(BF16)
    imp_t = (_dot(p_hi, ov_ref[...]) + _dot(p_lo, ov_ref[...])).T

    imp_ref[...] = imp_t
    bt_ref[...] = jnp.full_like(bt_ref, MASK_BIAS)

    def select_rows(nrows):
        blk = lax.broadcasted_iota(jnp.int32, (nrows, tq), 0)
        t_row = s0 + lax.broadcasted_iota(jnp.int32, (nrows, tq), 1)
        cur = lax.shift_right_logical(t_row, SEL_BLOCK.bit_length() - 1)
        forced = (blk == 0) | (blk == cur) | (blk == cur - 1)
        cand = jnp.where(forced | (blk * SEL_BLOCK > t_row), -FORCE_BONUS, imp_ref[0:nrows, :])
        ridx = blk.astype(F32)
        sc = cand
        for _ in range(n_sel - 3):
            mx = jnp.max(sc, axis=0, keepdims=True)
            idx = jnp.min(jnp.where(sc == mx, ridx, float(LANES)), axis=0, keepdims=True)
            sc = jnp.where(ridx == idx, -jnp.inf, sc)
        sel = forced | ((sc == -jnp.inf) & (cand > -1.0))
        bt_ref[0:nrows, :] = jnp.where(sel, 0.0, MASK_BIAS)

    n_live = (s0 + tq) // SEL_BLOCK
    lo, mid = LANES // 4, LANES // 2
    pl.when(n_live <= lo)(lambda: select_rows(lo))
    pl.when((n_live > lo) & (n_live <= mid))(lambda: select_rows(mid))
    pl.when(n_live > mid)(lambda: select_rows(LANES))
    bias = bt_ref[...].T.astype(BF16)
    for h in range(hpg):
        qa_ref[h, :, dh:2 * dh] = bias

    m_ref[...] = jnp.full_like(m_ref, MASK_BIAS)
    acc_ref[...] = jnp.zeros_like(acc_ref)

    def scores(j, h):
        return _dot_nt(qa_ref[h], ka_ref[pl.ds(pl.multiple_of(j * tk, tk), tk), :])

    def consume(s, j, h):
        m_prev = m_ref[h]
        m_new = jnp.maximum(m_prev, jnp.max(s, axis=-1, keepdims=True))
        alpha = jnp.exp2(m_prev - m_new)
        p = jnp.exp2(s - pltpu.repeat(m_new, tk // LANES, axis=1))
        pv = _dot(p.astype(BF16), ve_ref[pl.ds(pl.multiple_of(j * tk, tk), tk), :])
        acc_ref[h] = acc_ref[h] * pltpu.repeat(alpha, 2, axis=1) + pv
        m_ref[h] = m_new

    for h in range(hpg):
        s_ref[h] = scores(0, h)
    j_last = s0 // tk

    def body(j, carry):
        for h in range(hpg):
            s_next = scores(j + 1, h)
            consume(s_ref[h], j, h)
            s_ref[h] = s_next
        return carry

    lax.fori_loop(0, j_last, body, 0)
    causal = j_last * tk + lax.broadcasted_iota(jnp.int32, (tq, tk), 1) <= t_col
    for h in range(hpg):
        consume(jnp.where(causal, s_ref[h], MASK_BIAS), j_last, h)

    wk = WINDOW + tq
    start = pl.multiple_of(jnp.maximum(s0 - WINDOW, 0), tq)
    kpos = start + lax.broadcasted_iota(jnp.int32, (tq, wk), 1)
    mask_w = (kpos <= t_col) & (kpos > t_col - WINDOW)
    gates = jax.nn.sigmoid(gl_ref[...].astype(F32))
    for h in range(hpg):
        s_w = jnp.where(mask_w, _dot_nt(qa_ref[h, :, 0:dh], kw_ref[pl.ds(start, wk), :]), -jnp.inf)
        p_w = jnp.exp2(s_w - jnp.max(s_w, axis=-1, keepdims=True))
        r_w = 1.0 / jnp.sum(p_w, axis=-1, keepdims=True)
        o_w = _dot(p_w.astype(BF16), vw_ref[pl.ds(start, wk), :]) * r_w
        acc = acc_ref[h]
        o_s = acc[:, 0:dh] * (1.0 / acc[:, dh:dh + 1])
        out = (gates[:, h:h + 1] * o_c[h]
               + gates[:, hpg + h:hpg + h + 1] * o_s
               + gates[:, 2 * hpg + h:2 * hpg + h + 1] * o_w)
        o_ref[:, h * dh:(h + 1) * dh] = out.astype(o_ref.dtype)


def _nsa(P, cmp_kv, overlap, lay, *, B, S, tq=512, tk=512):
    G, dh, hpg = N_KV_GROUPS, HEAD_DIM, HEADS_PER_GROUP
    tq = min(tq, S)
    tk = min(tk, S)
    nk = cmp_kv.shape[3]
    n_sel = min(SEL_TOPK, S // SEL_BLOCK)
    assert S % tq == 0 and tk % tq == 0 and S % tk == 0 and S >= WINDOW + tq
    assert S // SEL_BLOCK <= LANES and nk % LANES == 0
    nq = S // tq
    qw = hpg * dh
    kern = functools.partial(_nsa_kernel, tq=tq, tk=tk, n_sel=n_sel)
    return pl.pallas_call(
        kern,
        grid=(B, G, nq),
        in_specs=[
            pl.BlockSpec((tq, qw), lambda b, g, i: (b * nq + i, lay["q"] // qw + g)),
            pl.BlockSpec((tq, LANES), lambda b, g, i: (b * nq + i, lay["gates"] // LANES + g)),
            pl.BlockSpec((None, None, None, nk, dh), lambda b, g, i: (0, b, g, 0, 0)),
            pl.BlockSpec((None, None, None, nk, dh), lambda b, g, i: (1, b, g, 0, 0)),
            pl.BlockSpec((nk, LANES), lambda b, g, i: (0, 0)),
            pl.BlockSpec((S, dh), lambda b, g, i: (b, lay["ks"] // dh + g)),
            pl.BlockSpec((S, dh), lambda b, g, i: (b, lay["vs"] // dh + g)),
            pl.BlockSpec((S, dh), lambda b, g, i: (b, lay["kw"] // dh + g)),
            pl.BlockSpec((S, dh), lambda b, g, i: (b, lay["vw"] // dh + g)),
        ],
        out_specs=pl.BlockSpec((tq, qw), lambda b, g, i: (b * nq + i, g)),
        out_shape=jax.ShapeDtypeStruct((B * S, G * qw), BF16),
        scratch_shapes=[
            pltpu.VMEM((S, 2 * dh), BF16),
            pltpu.VMEM((S, 2 * dh), BF16),
            pltpu.VMEM((hpg, tq, 2 * dh), BF16),
            pltpu.VMEM((hpg, tq, LANES), F32),
            pltpu.VMEM((hpg, tq, 2 * dh), F32),
            pltpu.VMEM((hpg, tq, tk), F32),
            pltpu.VMEM((LANES, tq), F32),
            pltpu.VMEM((LANES, tq), F32),
        ],
        compiler_params=_params("parallel", "parallel", "arbitrary"),
        name="nsa",
    )(P, P, cmp_kv, cmp_kv, overlap, P, P, P, P)


def _mix_kernel(xp_ref, halo_ref, b_ref, gp_ref, gn_ref, pw_ref, ps_ref, wpu_ref, wnu_ref, o_ref, *, tm, S):
    halo_rows = halo_ref.shape[0]
    gd = POOL_GROUP_DIM
    t0 = (pl.program_id(0) * tm) % S
    x = xp_ref[...].astype(F32)
    halo = halo_ref[...].astype(F32) * (t0 > 0).astype(F32)
    xc = jnp.concatenate([halo, x], axis=0)
    tpos = t0 + lax.broadcasted_iota(jnp.int32, (tm, 1), 0)
    a_up = jnp.zeros((tm, o_ref.shape[1]), F32)
    for gi, win in enumerate(POOL_WINDOWS):
        cols = slice(gi * gd, (gi + 1) * gd)
        s = xc[:, cols]
        d = 1
        while d < win:
            s = s + pltpu.roll(s, d, 0)
            d *= 2
        cnt = jnp.minimum(tpos + 1, win).astype(F32)
        pooled = s[halo_rows:] / cnt - x[:, cols]
        y = _dot(pooled.astype(BF16), pw_ref[gi]) * ps_ref[:, cols]
        a_up = a_up + _dot(y.astype(BF16), wpu_ref[cols, :])
    b_up = _dot(b_ref[...], wnu_ref[...])
    merged = (jax.nn.sigmoid(gp_ref[...].astype(F32)) * a_up
              + jax.nn.sigmoid(gn_ref[...].astype(F32)) * b_up)
    o_ref[...] = merged.astype(o_ref.dtype)


def _mix(P, b_attn, pool_w, pool_scale, w_pool_up, w_nsa_up, lay, *, S, tm=256):
    T = P.shape[0]
    pw = pool_w.shape[0] * pool_w.shape[1]
    D = w_pool_up.shape[1]
    halo = max(POOL_WINDOWS)
    assert S % tm == 0 and tm % halo == 0 and lay["pool"] == 0
    kern = functools.partial(_mix_kernel, tm=tm, S=S)
    return pl.pallas_call(
        kern,
        grid=(T // tm,),
        in_specs=[
            pl.BlockSpec((tm, pw), lambda i: (i, 0)),
            pl.BlockSpec((halo, pw), lambda i: (jnp.maximum(i * (tm // halo) - 1, 0), 0)),
            pl.BlockSpec((tm, b_attn.shape[1]), lambda i: (i, 0)),
            pl.BlockSpec((tm, D), lambda i: (i, lay["g_pool"] // D)),
            pl.BlockSpec((tm, D), lambda i: (i, lay["g_nsa"] // D)),
            pl.BlockSpec(pool_w.shape, lambda i: (0, 0, 0)),
            pl.BlockSpec((1, pw), lambda i: (0, 0)),
            pl.BlockSpec(w_pool_up.shape, lambda i: (0, 0)),
            pl.BlockSpec(w_nsa_up.shape, lambda i: (0, 0)),
        ],
        out_specs=pl.BlockSpec((tm, D), lambda i: (i, 0)),
        out_shape=jax.ShapeDtypeStruct((T, D), BF16),
        compiler_params=_params("parallel"),
        name="mix",
    )(P, P, b_attn, P, P, pool_w, pool_scale.reshape(1, pw), w_pool_up, w_nsa_up)


def _outproj_kernel(h_ref, m_ref, w_ref, o_ref):
    o_ref[...] = h_ref[...] + _dot(m_ref[...], w_ref[...])


def _outproj(h, merged, w_out, *, tm=512):
    T, D = h.shape
    assert T % tm == 0
    return pl.pallas_call(
        _outproj_kernel,
        grid=(T // tm,),
        in_specs=[
            pl.BlockSpec((tm, D), lambda i: (i, 0)),
            pl.BlockSpec((tm, D), lambda i: (i, 0)),
            pl.BlockSpec(w_out.shape, lambda i: (0, 0)),
        ],
        out_specs=pl.BlockSpec((tm, D), lambda i: (i, 0)),
        out_shape=jax.ShapeDtypeStruct((T, D), F32),
        compiler_params=_params("parallel"),
        name="out_proj",
    )(h, merged, w_out)


def _prep_w_in(w_in, pool_width, d_model):
    G, hpg, dh = N_KV_GROUPS, HEADS_PER_GROUP, HEAD_DIM
    qw, kvw = N_HEADS * dh, G * dh
    o_q = pool_width
    o_kv = o_q + qw
    o_gate = o_kv + 6 * kvw
    o_gp = o_gate + N_NSA_BRANCHES * N_HEADS
    o_gn = o_gp + d_model
    assert w_in.shape[1] == o_gn + d_model
    gate_cols = w_in[:, o_gate:o_gp].reshape(-1, N_NSA_BRANCHES, G, hpg)
    gate_cols = jnp.transpose(gate_cols, (0, 2, 1, 3)).reshape(-1, G, N_NSA_BRANCHES * hpg)
    gate_cols = jnp.pad(gate_cols, ((0, 0), (0, 0), (0, LANES - N_NSA_BRANCHES * hpg))).reshape(-1, G * LANES)
    w = jnp.concatenate([
        w_in[:, :o_q],
        w_in[:, o_q:o_kv] * (LOG2E * dh ** -0.5),
        w_in[:, o_kv:o_gate],
        w_in[:, o_gp:o_gn],
        w_in[:, o_gn:],
        gate_cols,
    ], axis=1).astype(BF16)
    lay = {"pool": 0, "q": o_q, "kc": o_kv, "vc": o_kv + kvw, "ks": o_kv + 2 * kvw, "vs": o_kv + 3 * kvw,
           "kw": o_kv + 4 * kvw, "vw": o_kv + 5 * kvw, "g_pool": o_gate, "g_nsa": o_gate + d_model,
           "gates": o_gate + 2 * d_model}
    return w, lay


def _overlap_matrix(nk, n_cmp, n_blk):
    n = jnp.arange(nk)[:, None]
    m = jnp.arange(LANES)[None, :]
    cs = n * CMP_STRIDE
    ss = m * SEL_BLOCK
    ov = (cs <= ss + SEL_BLOCK - 1) & (cs + CMP_BLOCK - 1 >= ss) & (n < n_cmp) & (m < n_blk)
    return ov.astype(BF16)


def kernel(x, ffn1_norm, ffn1_w_gate, ffn1_w_up, ffn1_w_down, mix_norm, w_in, pool_w, pool_scale,
           cmp_pe_k, cmp_pe_v, cmp_k_w1, cmp_k_w2, cmp_v_w1, cmp_v_w2, w_pool_up, w_nsa_up, w_out,
           ffn2_norm, ffn2_w_gate, ffn2_w_up, ffn2_w_down, final_norm):
    B, S, D = x.shape
    depth = ffn1_norm.shape[0]
    G, dh = N_KV_GROUPS, HEAD_DIM
    T = B * S
    pool_width = pool_w.shape[1] * pool_w.shape[2]
    n_chunk = S // CMP_STRIDE
    n_cmp = n_chunk - CMP_BLOCK // CMP_STRIDE + 1
    n_blk = S // SEL_BLOCK
    cw = CMP_STRIDE * dh

    h = x.reshape(T, D)
    for l in range(depth):
        last = l == depth - 1
        h = _ffn(h, ffn1_norm[l], ffn1_w_gate[l].astype(BF16), ffn1_w_up[l].astype(BF16),
                 ffn1_w_down[l].astype(BF16))

        w_p, lay = _prep_w_in(w_in[l], pool_width, D)
        P = _proj(h, mix_norm[l], w_p)

        w1 = jnp.stack([cmp_k_w1[l], cmp_v_w1[l]]).astype(BF16)
        w2 = jnp.stack([cmp_k_w2[l], cmp_v_w2[l]]).astype(BF16)
        pe = jnp.stack([cmp_pe_k[l], cmp_pe_v[l]]).reshape(2, 1, CMP_BLOCK * dh).astype(BF16)
        cmp_kv = _compress(P, w1, w2, pe, lay, B=B, S=S)

        overlap = _overlap_matrix(n_chunk, n_cmp, n_blk)

        b_attn = _nsa(P, cmp_kv, overlap, lay, B=B, S=S)
        merged = _mix(P, b_attn, pool_w[l].astype(BF16), pool_scale[l], w_pool_up[l].astype(BF16),
                      w_nsa_up[l].astype(BF16), lay, S=S)
        h = _outproj(h, merged, w_out[l].astype(BF16))
        h = _ffn(h, ffn2_norm[l], ffn2_w_gate[l].astype(BF16), ffn2_w_up[l].astype(BF16),
                 ffn2_w_down[l].astype(BF16), final_norm if last else None)
    if depth == 0:
        raise ValueError("depth must be positive")
    return h.reshape(B, S, D)
```

```python
import functools

import jax
import jax.numpy as jnp
from jax import lax
from jax.experimental import pallas as pl
from jax.experimental.pallas import tpu as pltpu

F32 = jnp.float32
BF16 = jnp.bfloat16

POOL_WINDOWS = (2, 4, 8, 16)
POOL_GROUP_DIM = 256
N_HEADS = 16
N_KV_GROUPS = 4
HEADS_PER_GROUP = N_HEADS // N_KV_GROUPS
HEAD_DIM = 128
CMP_BLOCK = 32
CMP_STRIDE = 16
SEL_BLOCK = 64
SEL_TOPK = 16
WINDOW = 512
N_NSA_BRANCHES = 3
FORCE_BONUS = 1000.0
RMS_EPS = 1e-6

LOG2E = 1.4426950408889634
LANES = 128
MASK_BIAS = -1e30
VMEM_LIMIT = 56 * 1024 * 1024


def _dot(a, b):
    return jnp.dot(a, b, preferred_element_type=F32)


def _dot_nt(a, b):
    return lax.dot_general(a, b, (((1,), (1,)), ((), ())), preferred_element_type=F32)


def _rms(x, g):
    ms = jnp.mean(x * x, axis=-1, keepdims=True)
    return x * lax.rsqrt(ms + RMS_EPS) * g


def _params(*sem):
    return pltpu.CompilerParams(dimension_semantics=sem, vmem_limit_bytes=VMEM_LIMIT)


def _ffn_kernel(*refs, final):
    if final:
        x_ref, g_ref, wg_ref, wu_ref, wd_ref, fg_ref, o_ref, xn_ref, acc_ref = refs
    else:
        x_ref, g_ref, wg_ref, wu_ref, wd_ref, o_ref, xn_ref, acc_ref = refs
    j = pl.program_id(1)

    @pl.when(j == 0)
    def _():
        xn_ref[...] = _rms(x_ref[...], g_ref[...]).astype(BF16)
        acc_ref[...] = jnp.zeros_like(acc_ref)

    xn = xn_ref[...]
    half = wg_ref.shape[1] // 2
    for c in range(2):
        cols = slice(c * half, (c + 1) * half)
        gate = _dot(xn, wg_ref[:, cols])
        up = _dot(xn, wu_ref[:, cols])
        act = (gate * jax.nn.sigmoid(gate) * up).astype(BF16)
        acc_ref[...] += _dot(act, wd_ref[cols, :])

    @pl.when(j == pl.num_programs(1) - 1)
    def _():
        h = x_ref[...] + 0.5 * acc_ref[...]
        if final:
            h = _rms(h, fg_ref[...])
        o_ref[...] = h


def _ffn(x, g, wg, wu, wd, final_g=None, *, tm=512, tf=512):
    T, D = x.shape
    F = wg.shape[1]
    assert T % tm == 0 and F % tf == 0
    final = final_g is not None
    in_specs = [
        pl.BlockSpec((tm, D), lambda i, j: (i, 0)),
        pl.BlockSpec((1, D), lambda i, j: (0, 0)),
        pl.BlockSpec((D, tf), lambda i, j: (0, j)),
        pl.BlockSpec((D, tf), lambda i, j: (0, j)),
        pl.BlockSpec((tf, D), lambda i, j: (j, 0)),
    ]
    args = [x, g.reshape(1, D), wg, wu, wd]
    if final:
        in_specs.append(pl.BlockSpec((1, D), lambda i, j: (0, 0)))
        args.append(final_g.reshape(1, D))
    return pl.pallas_call(
        functools.partial(_ffn_kernel, final=final),
        grid=(T // tm, F // tf),
        in_specs=in_specs,
        out_specs=pl.BlockSpec((tm, D), lambda i, j: (i, 0)),
        out_shape=jax.ShapeDtypeStruct((T, D), F32),
        scratch_shapes=[pltpu.VMEM((tm, D), BF16), pltpu.VMEM((tm, D), F32)],
        compiler_params=_params("parallel", "arbitrary"),
        name="ffn_final" if final else "ffn",
    )(*args)


def _proj_kernel(x_ref, g_ref, w_ref, o_ref, xn_ref):
    @pl.when(pl.program_id(1) == 0)
    def _():
        xn_ref[...] = _rms(x_ref[...], g_ref[...]).astype(BF16)

    o_ref[...] = _dot(xn_ref[...], w_ref[...]).astype(o_ref.dtype)


def _proj(x, g, w, *, tm=1024, tn=1536):
    T, D = x.shape
    N = w.shape[1]
    tm = min(tm, T)
    assert T % tm == 0 and N % tn == 0
    return pl.pallas_call(
        _proj_kernel,
        grid=(T // tm, N // tn),
        in_specs=[
            pl.BlockSpec((tm, D), lambda i, j: (i, 0)),
            pl.BlockSpec((1, D), lambda i, j: (0, 0)),
            pl.BlockSpec((D, tn), lambda i, j: (0, j)),
        ],
        out_specs=pl.BlockSpec((tm, tn), lambda i, j: (i, j)),
        out_shape=jax.ShapeDtypeStruct((T, N), BF16),
        scratch_shapes=[pltpu.VMEM((tm, D), BF16)],
        compiler_params=_params("parallel", "arbitrary"),
        name="in_proj",
    )(x, g.reshape(1, D), w)


def _cmp_kernel(t_ref, w1_ref, w2_ref, pe_ref, o_ref, xf_ref, c_ref):
    seq, dh = t_ref.shape
    nch, half = c_ref.shape
    xf_ref[...] = t_ref[...].astype(F32)
    for j in range(CMP_STRIDE):
        c_ref[:, j * dh:(j + 1) * dh] = xf_ref[pl.ds(j, nch, stride=CMP_STRIDE), :].astype(BF16)
    c = c_ref[...]
    a = _dot(c, w1_ref[:half, :])
    b = _dot(c, w1_ref[half:, :])
    pe = jnp.broadcast_to(pe_ref[...], (8, 2 * half))
    pe_term = _dot(pe, w1_ref[...])[0:1]
    pre = a + pltpu.roll(b, nch - 1, 0) + pe_term
    hid = (pre * jax.nn.sigmoid(pre)).astype(BF16)
    o_ref[...] = _dot(hid, w2_ref[...]).astype(o_ref.dtype)


def _compress(P, w1, w2, pe, lay, *, B, S):
    G, dh = N_KV_GROUPS, HEAD_DIM
    nch = S // CMP_STRIDE
    cw = CMP_STRIDE * dh
    hid = w1.shape[2]
    kc_blk, kv_stride = lay["kc"] // dh, (lay["vc"] - lay["kc"]) // dh
    return pl.pallas_call(
        _cmp_kernel,
        grid=(2, B, G),
        in_specs=[
            pl.BlockSpec((S, dh), lambda s, b, g: (b, kc_blk + s * kv_stride + g)),
            pl.BlockSpec((None, 2 * cw, hid), lambda s, b, g: (s, 0, 0)),
            pl.BlockSpec((None, hid, dh), lambda s, b, g: (s, 0, 0)),
            pl.BlockSpec((None, 1, 2 * cw), lambda s, b, g: (s, 0, 0)),
        ],
        out_specs=pl.BlockSpec((None, None, None, nch, dh), lambda s, b, g: (s, b, g, 0, 0)),
        out_shape=jax.ShapeDtypeStruct((2, B, G, nch, dh), BF16),
        scratch_shapes=[pltpu.VMEM((S, dh), F32), pltpu.VMEM((nch, cw), BF16)],
        compiler_params=_params("parallel", "parallel", "parallel"),
        name="compress",
    )(P, w1, w2, pe)


def _nsa_kernel(q_ref, gl_ref, kc_ref, vc_ref, ov_ref, ks_ref, vs_ref, kw_ref, vw_ref, o_ref,
                ka_ref, ve_ref, qa_ref, m_ref, acc_ref, s_ref, imp_ref, bt_ref, *, tq, tk, n_sel):
    hpg, dh = HEADS_PER_GROUP, HEAD_DIM
    s0 = pl.program_id(2) * tq
    t_col = s0 + lax.broadcasted_iota(jnp.int32, (tq, 1), 0)

    @pl.when(pl.program_id(2) == 0)
    def _():
        seq = ks_ref.shape[0]
        for c in range(seq // tk):
            r = slice(c * tk, (c + 1) * tk)
            kblk = lax.shift_right_logical(c * tk + lax.broadcasted_iota(jnp.int32, (tk, LANES), 0),
                                           SEL_BLOCK.bit_length() - 1)
            lane = lax.broadcasted_iota(jnp.int32, (tk, LANES), 1)
            ka_ref[r, 0:dh] = ks_ref[r, :]
            ka_ref[r, dh:2 * dh] = (kblk == lane).astype(BF16)
            ve_ref[r, 0:dh] = vs_ref[r, :]
            ve_ref[r, dh:2 * dh] = (lane == 0).astype(BF16)

    for h in range(hpg):
        qa_ref[h, :, 0:dh] = q_ref[:, h * dh:(h + 1) * dh]

    nk = kc_ref.shape[0]
    n_idx = lax.broadcasted_iota(jnp.int32, (tq, nk), 1)
    mask_c = (n_idx * CMP_STRIDE + (CMP_BLOCK - 1)) <= t_col
    o_c = []
    p_sum = None
    for h in range(hpg):
        s_c = jnp.where(mask_c, _dot_nt(qa_ref[h, :, 0:dh], kc_ref[...]), -jnp.inf)
        m_c = jnp.max(s_c, axis=-1, keepdims=True)
        m_c = jnp.where(m_c == -jnp.inf, 0.0, m_c)
        p_c = jnp.exp2(s_c - m_c)
        p_c = p_c * (1.0 / jnp.maximum(jnp.sum(p_c, axis=-1, keepdims=True), 1e-30))
        o_c.append(_dot(p_c.astype(BF16), vc_ref[...]))
        p_sum = p_c if p_sum is None else p_sum + p_c

    p_hi = p_sum.astype(BF16)
    p_lo = (p_sum - p_hi.astype(F32)).astype(BF16)
    imp_t = (_dot(p_hi, ov_ref[...]) + _dot(p_lo, ov_ref[...])).T

    imp_ref[...] = imp_t
    bt_ref[...] = jnp.full_like(bt_ref, MASK_BIAS)

    def select_rows(nrows):
        blk = lax.broadcasted_iota(jnp.int32, (nrows, tq), 0)
        t_row = s0 + lax.broadcasted_iota(jnp.int32, (nrows, tq), 1)
        cur = lax.shift_right_logical(t_row, SEL_BLOCK.bit_length() - 1)
        forced = (blk == 0) | (blk == cur) | (blk == cur - 1)
        cand = jnp.where(forced | (blk * SEL_BLOCK > t_row), -FORCE_BONUS, imp_ref[0:nrows, :])
        ridx = blk.astype(F32)
        sc = cand
        for _ in range(n_sel - 3):
            mx = jnp.max(sc, axis=0, keepdims=True)
            idx = jnp.min(jnp.where(sc == mx, ridx, float(LANES)), axis=0, keepdims=True)
            sc = jnp.where(ridx == idx, -jnp.inf, sc)
        sel = forced | ((sc == -jnp.inf) & (cand > -1.0))
        bt_ref[0:nrows, :] = jnp.where(sel, 0.0, MASK_BIAS)

    n_live = (s0 + tq) // SEL_BLOCK
    lo, mid = LANES // 4, LANES // 2
    pl.when(n_live <= lo)(lambda: select_rows(lo))
    pl.when((n_live > lo) & (n_live <= mid))(lambda: select_rows(mid))
    pl.when(n_live > mid)(lambda: select_rows(LANES))
    bias = bt_ref[...].T.astype(BF16)
    for h in range(hpg):
        qa_ref[h, :, dh:2 * dh] = bias

    m_ref[...] = jnp.full_like(m_ref, MASK_BIAS)
    acc_ref[...] = jnp.zeros_like(acc_ref)

    def scores(j, h):
        return _dot_nt(qa_ref[h], ka_ref[pl.ds(pl.multiple_of(j * tk, tk), tk), :])

    def consume(s, j, h):
        m_prev = m_ref[h]
        m_new = jnp.maximum(m_prev, jnp.max(s, axis=-1, keepdims=True))
        alpha = jnp.exp2(m_prev - m_new)
        p = jnp.exp2(s - pltpu.repeat(m_new, tk // LANES, axis=1))
        pv = _dot(p.astype(BF16), ve_ref[pl.ds(pl.multiple_of(j * tk, tk), tk), :])
        acc_ref[h] = acc_ref[h] * pltpu.repeat(alpha, 2, axis=1) + pv
        m_ref[h] = m_new

    for h in range(hpg):
        s_ref[h] = scores(0, h)
    j_last = s0 // tk

    def body(j, carry):
        for h in range(hpg):
            s_next = scores(j + 1, h)
            consume(s_ref[h], j, h)
            s_ref[h] = s_next
        return carry

    lax.fori_loop(0, j_last, body, 0)
    causal = j_last * tk + lax.broadcasted_iota(jnp.int32, (tq, tk), 1) <= t_col
    for h in range(hpg):
        consume(jnp.where(causal, s_ref[h], MASK_BIAS), j_last, h)

    wk = WINDOW + tq
    start = pl.multiple_of(jnp.maximum(s0 - WINDOW, 0), tq)
    kpos = start + lax.broadcasted_iota(jnp.int32, (tq, wk), 1)
    mask_w = (kpos <= t_col) & (kpos > t_col - WINDOW)
    gates = jax.nn.sigmoid(gl_ref[...].astype(F32))
    for h in range(hpg):
        s_w = jnp.where(mask_w, _dot_nt(qa_ref[h, :, 0:dh], kw_ref[pl.ds(start, wk), :]), -jnp.inf)
        p_w = jnp.exp2(s_w - jnp.max(s_w, axis=-1, keepdims=True))
        r_w = 1.0 / jnp.sum(p_w, axis=-1, keepdims=True)
        o_w = _dot(p_w.astype(BF16), vw_ref[pl.ds(start, wk), :]) * r_w
        acc = acc_ref[h]
        o_s = acc[:, 0:dh] * (1.0 / acc[:, dh:dh + 1])
        out = (gates[:, h:h + 1] * o_c[h]
               + gates[:, hpg + h:hpg + h + 1] * o_s
               + gates[:, 2 * hpg + h:2 * hpg + h + 1] * o_w)
        o_ref[:, h * dh:(h + 1) * dh] = out.astype(o_ref.dtype)


def _nsa(P, cmp_kv, overlap, lay, *, B, S, tq=512, tk=512):
    G, dh, hpg = N_KV_GROUPS, HEAD_DIM, HEADS_PER_GROUP
    tq = min(tq, S)
    tk = min(tk, S)
    nk = cmp_kv.shape[3]
    n_sel = min(SEL_TOPK, S // SEL_BLOCK)
    assert S % tq == 0 and tk % tq == 0 and S % tk == 0 and S >= WINDOW + tq
    assert S // SEL_BLOCK <= LANES and nk % LANES == 0
    nq = S // tq
    qw = hpg * dh
    kern = functools.partial(_nsa_kernel, tq=tq, tk=tk, n_sel=n_sel)
    return pl.pallas_call(
        kern,
        grid=(B, G, nq),
        in_specs=[
            pl.BlockSpec((tq, qw), lambda b, g, i: (b * nq + i, lay["q"] // qw + g)),
            pl.BlockSpec((tq, LANES), lambda b, g, i: (b * nq + i, lay["gates"] // LANES + g)),
            pl.BlockSpec((None, None, None, nk, dh), lambda b, g, i: (0, b, g, 0, 0)),
            pl.BlockSpec((None, None, None, nk, dh), lambda b, g, i: (1, b, g, 0, 0)),
            pl.BlockSpec((nk, LANES), lambda b, g, i: (0, 0)),
            pl.BlockSpec((S, dh), lambda b, g, i: (b, lay["ks"] // dh + g)),
            pl.BlockSpec((S, dh), lambda b, g, i: (b, lay["vs"] // dh + g)),
            pl.BlockSpec((S, dh), lambda b, g, i: (b, lay["kw"] // dh + g)),
            pl.BlockSpec((S, dh), lambda b, g, i: (b, lay["vw"] // dh + g)),
        ],
        out_specs=pl.BlockSpec((tq, qw), lambda b, g, i: (b * nq + i, g)),
        out_shape=jax.ShapeDtypeStruct((B * S, G * qw), BF16),
        scratch_shapes=[
            pltpu.VMEM((S, 2 * dh), BF16),
            pltpu.VMEM((S, 2 * dh), BF16),
            pltpu.VMEM((hpg, tq, 2 * dh), BF16),
            pltpu.VMEM((hpg, tq, LANES), F32),
            pltpu.VMEM((hpg, tq, 2 * dh), F32),
            pltpu.VMEM((hpg, tq, tk), F32),
            pltpu.VMEM((LANES, tq), F32),
            pltpu.VMEM((LANES, tq), F32),
        ],
        compiler_params=_params("parallel", "parallel", "arbitrary"),
        name="nsa",
    )(P, P, cmp_kv, cmp_kv, overlap, P, P, P, P)


def _mix_kernel(xp_ref, halo_ref, b_ref, gp_ref, gn_ref, pw_ref, ps_ref, wpu_ref, wnu_ref, o_ref, *, tm, S):
    halo_rows = halo_ref.shape[0]
    gd = POOL_GROUP_DIM
    t0 = (pl.program_id(0) * tm) % S
    x = xp_ref[...].astype(F32)
    halo = halo_ref[...].astype(F32) * (t0 > 0).astype(F32)
    xc = jnp.concatenate([halo, x], axis=0)
    tpos = t0 + lax.broadcasted_iota(jnp.int32, (tm, 1), 0)
    a_up = jnp.zeros((tm, o_ref.shape[1]), F32)
    for gi, win in enumerate(POOL_WINDOWS):
        cols = slice(gi * gd, (gi + 1) * gd)
        s = xc[:, cols]
        d = 1
        while d < win:
            s = s + pltpu.roll(s, d, 0)
            d *= 2
        cnt = jnp.minimum(tpos + 1, win).astype(F32)
        pooled = s[halo_rows:] / cnt - x[:, cols]
        y = _dot(pooled.astype(BF16), pw_ref[gi]) * ps_ref[:, cols]
        a_up = a_up + _dot(y.astype(BF16), wpu_ref[cols, :])
    b_up = _dot(b_ref[...], wnu_ref[...])
    merged = (jax.nn.sigmoid(gp_ref[...].astype(F32)) * a_up
              + jax.nn.sigmoid(gn_ref[...].astype(F32)) * b_up)
    o_ref[...] = merged.astype(o_ref.dtype)


def _mix(P, b_attn, pool_w, pool_scale, w_pool_up, w_nsa_up, lay, *, S, tm=256):
    T = P.shape[0]
    pw = pool_w.shape[0] * pool_w.shape[1]
    D = w_pool_up.shape[1]
    halo = max(POOL_WINDOWS)
    assert S % tm == 0 and tm % halo == 0 and lay["pool"] == 0
    kern = functools.partial(_mix_kernel, tm=tm, S=S)
    return pl.pallas_call(
        kern,
        grid=(T // tm,),
        in_specs=[
            pl.BlockSpec((tm, pw), lambda i: (i, 0)),
            pl.BlockSpec((halo, pw), lambda i: (jnp.maximum(i * (tm // halo) - 1, 0), 0)),
            pl.BlockSpec((tm, b_attn.shape[1]), lambda i: (i, 0)),
            pl.BlockSpec((tm, D), lambda i: (i, lay["g_pool"] // D)),
            pl.BlockSpec((tm, D), lambda i: (i, lay["g_nsa"] // D)),
            pl.BlockSpec(pool_w.shape, lambda i: (0, 0, 0)),
            pl.BlockSpec((1, pw), lambda i: (0, 0)),
            pl.BlockSpec(w_pool_up.shape, lambda i: (0, 0)),
            pl.BlockSpec(w_nsa_up.shape, lambda i: (0, 0)),
        ],
        out_specs=pl.BlockSpec((tm, D), lambda i: (i, 0)),
        out_shape=jax.ShapeDtypeStruct((T, D), BF16),
        compiler_params=_params("parallel"),
        name="mix",
    )(P, P, b_attn, P, P, pool_w, pool_scale.reshape(1, pw), w_pool_up, w_nsa_up)


def _outproj_kernel(h_ref, m_ref, w_ref, o_ref):
    o_ref[...] = h_ref[...] + _dot(m_ref[...], w_ref[...])


def _outproj(h, merged, w_out, *, tm=512):
    T, D = h.shape
    assert T % tm == 0
    return pl.pallas_call(
        _outproj_kernel,
        grid=(T // tm,),
        in_specs=[
            pl.BlockSpec((tm, D), lambda i: (i, 0)),
            pl.BlockSpec((tm, D), lambda i: (i, 0)),
            pl.BlockSpec(w_out.shape, lambda i: (0, 0)),
        ],
        out_specs=pl.BlockSpec((tm, D), lambda i: (i, 0)),
        out_shape=jax.ShapeDtypeStruct((T, D), F32),
        compiler_params=_params("parallel"),
        name="out_proj",
    )(h, merged, w_out)


def _prep_w_in(w_in, pool_width, d_model):
    G, hpg, dh = N_KV_GROUPS, HEADS_PER_GROUP, HEAD_DIM
    qw, kvw = N_HEADS * dh, G * dh
    o_q = pool_width
    o_kv = o_q + qw
    o_gate = o_kv + 6 * kvw
    o_gp = o_gate + N_NSA_BRANCHES * N_HEADS
    o_gn = o_gp + d_model
    assert w_in.shape[1] == o_gn + d_model
    gate_cols = w_in[:, o_gate:o_gp].reshape(-1, N_NSA_BRANCHES, G, hpg)
    gate_cols = jnp.transpose(gate_cols, (0, 2, 1, 3)).reshape(-1, G, N_NSA_BRANCHES * hpg)
    gate_cols = jnp.pad(gate_cols, ((0, 0), (0, 0), (0, LANES - N_NSA_BRANCHES * hpg))).reshape(-1, G * LANES)
    pieces = [
        w_in[:, :o_q],
        w_in[:, o_q:o_kv] * (LOG2E * dh ** -0.5),
        w_in[:, o_kv:o_gate],
        w_in[:, o_gp:o_gn],
        w_in[:, o_gn:],
        gate_cols,
    ]
    w = jnp.concatenate([p.astype(BF16) for p in pieces], axis=1)
    lay = {"pool": 0, "q": o_q, "kc": o_kv, "vc": o_kv + kvw, "ks": o_kv + 2 * kvw, "vs": o_kv + 3 * kvw,
           "kw": o_kv + 4 * kvw, "vw": o_kv + 5 * kvw, "g_pool": o_gate, "g_nsa": o_gate + d_model,
           "gates": o_gate + 2 * d_model}
    return w, lay


def _overlap_matrix(nk, n_cmp, n_blk):
    n = jnp.arange(nk)[:, None]
    m = jnp.arange(LANES)[None, :]
    cs = n * CMP_STRIDE
    ss = m * SEL_BLOCK
    ov = (cs <= ss + SEL_BLOCK - 1) & (cs + CMP_BLOCK - 1 >= ss) & (n < n_cmp) & (m < n_blk)
    return ov.astype(BF16)


def kernel(x, ffn1_norm, ffn1_w_gate, ffn1_w_up, ffn1_w_down, mix_norm, w_in, pool_w, pool_scale,
           cmp_pe_k, cmp_pe_v, cmp_k_w1, cmp_k_w2, cmp_v_w1, cmp_v_w2, w_pool_up, w_nsa_up, w_out,
           ffn2_norm, ffn2_w_gate, ffn2_w_up, ffn2_w_down, final_norm):
    B, S, D = x.shape
    depth = ffn1_norm.shape[0]
    G, dh = N_KV_GROUPS, HEAD_DIM
    T = B * S
    pool_width = pool_w.shape[1] * pool_w.shape[2]
    n_chunk = S // CMP_STRIDE
    n_cmp = n_chunk - CMP_BLOCK // CMP_STRIDE + 1
    n_blk = S // SEL_BLOCK
    cw = CMP_STRIDE * dh

    h = x.reshape(T, D)
    for l in range(depth):
        last = l == depth - 1
        h = _ffn(h, ffn1_norm[l], ffn1_w_gate[l].astype(BF16), ffn1_w_up[l].astype(BF16),
                 ffn1_w_down[l].astype(BF16))

        w_p, lay = _prep_w_in(w_in[l], pool_width, D)
        P = _proj(h, mix_norm[l], w_p)

        w1 = jnp.stack([cmp_k_w1[l], cmp_v_w1[l]]).astype(BF16)
        w2 = jnp.stack([cmp_k_w2[l], cmp_v_w2[l]]).astype(BF16)
        pe = jnp.stack([cmp_pe_k[l], cmp_pe_v[l]]).reshape(2, 1, CMP_BLOCK * dh).astype(BF16)
        cmp_kv = _compress(P, w1, w2, pe, lay, B=B, S=S)

        overlap = _overlap_matrix(n_chunk, n_cmp, n_blk)

        b_attn = _nsa(P, cmp_kv, overlap, lay, B=B, S=S)
        merged = _mix(P, b_attn, pool_w[l].astype(BF16), pool_scale[l], w_pool_up[l].astype(BF16),
                      w_nsa_up[l].astype(BF16), lay, S=S)
        h = _outproj(h, merged, w_out[l].astype(BF16))
        h = _ffn(h, ffn2_norm[l], ffn2_w_gate[l].astype(BF16), ffn2_w_up[l].astype(BF16),
                 ffn2_w_down[l].astype(BF16), final_norm if last else None)
    if depth == 0:
        raise ValueError("depth must be positive")
    return h.reshape(B, S, D)
```

```python
import functools

import jax
import jax.numpy as jnp
from jax import lax
from jax.experimental import pallas as pl
from jax.experimental.pallas import tpu as pltpu

F32 = jnp.float32
BF16 = jnp.bfloat16

POOL_WINDOWS = (2, 4, 8, 16)
POOL_GROUP_DIM = 256
N_HEADS = 16
N_KV_GROUPS = 4
HEADS_PER_GROUP = N_HEADS // N_KV_GROUPS
HEAD_DIM = 128
CMP_BLOCK = 32
CMP_STRIDE = 16
SEL_BLOCK = 64
SEL_TOPK = 16
WINDOW = 512
N_NSA_BRANCHES = 3
FORCE_BONUS = 1000.0
RMS_EPS = 1e-6

LOG2E = 1.4426950408889634
LANES = 128
MASK_BIAS = -1e30
VMEM_LIMIT = 56 * 1024 * 1024


def _dot(a, b):
    return jnp.dot(a, b, preferred_element_type=F32)


def _dot_nt(a, b):
    return lax.dot_general(a, b, (((1,), (1,)), ((), ())), preferred_element_type=F32)


def _rms(x, g):
    ms = jnp.mean(x * x, axis=-1, keepdims=True)
    return x * lax.rsqrt(ms + RMS_EPS) * g


def _params(*sem):
    return pltpu.CompilerParams(dimension_semantics=sem, vmem_limit_bytes=VMEM_LIMIT)


def _ffn_kernel(*refs, final):
    if final:
        x_ref, g_ref, wg_ref, wu_ref, wd_ref, fg_ref, o_ref, xn_ref, acc_ref = refs
    else:
        x_ref, g_ref, wg_ref, wu_ref, wd_ref, o_ref, xn_ref, acc_ref = refs
    j = pl.program_id(1)

    @pl.when(j == 0)
    def _():
        xn_ref[...] = _rms(x_ref[...], g_ref[...]).astype(BF16)
        acc_ref[...] = jnp.zeros_like(acc_ref)

    xn = xn_ref[...]
    half = wg_ref.shape[1] // 2
    for c in range(2):
        cols = slice(c * half, (c + 1) * half)
        gate = _dot(xn, wg_ref[:, cols])
        up = _dot(xn, wu_ref[:, cols])
        act = (gate * jax.nn.sigmoid(gate) * up).astype(BF16)
        acc_ref[...] += _dot(act, wd_ref[cols, :])

    @pl.when(j == pl.num_programs(1) - 1)
    def _():
        h = x_ref[...] + 0.5 * acc_ref[...]
        if final:
            h = _rms(h, fg_ref[...])
        o_ref[...] = h


def _ffn(x, g, wg, wu, wd, final_g=None, *, tm=512, tf=512):
    T, D = x.shape
    F = wg.shape[1]
    assert T % tm == 0 and F % tf == 0
    final = final_g is not None
    in_specs = [
        pl.BlockSpec((tm, D), lambda i, j: (i, 0)),
        pl.BlockSpec((1, D), lambda i, j: (0, 0)),
        pl.BlockSpec((D, tf), lambda i, j: (0, j)),
        pl.BlockSpec((D, tf), lambda i, j: (0, j)),
        pl.BlockSpec((tf, D), lambda i, j: (j, 0)),
    ]
    args = [x, g.reshape(1, D), wg, wu, wd]
    if final:
        in_specs.append(pl.BlockSpec((1, D), lambda i, j: (0, 0)))
        args.append(final_g.reshape(1, D))
    return pl.pallas_call(
        functools.partial(_ffn_kernel, final=final),
        grid=(T // tm, F // tf),
        in_specs=in_specs,
        out_specs=pl.BlockSpec((tm, D), lambda i, j: (i, 0)),
        out_shape=jax.ShapeDtypeStruct((T, D), F32),
        scratch_shapes=[pltpu.VMEM((tm, D), BF16), pltpu.VMEM((tm, D), F32)],
        compiler_params=_params("parallel", "arbitrary"),
        name="ffn_final" if final else "ffn",
    )(*args)


def _proj_kernel(x_ref, g_ref, w_ref, o_ref, xn_ref):
    @pl.when(pl.program_id(1) == 0)
    def _():
        xn_ref[...] = _rms(x_ref[...], g_ref[...]).astype(BF16)

    o_ref[...] = _dot(xn_ref[...], w_ref[...]).astype(o_ref.dtype)


def _proj(x, g, w, *, tm=1024, tn=1536):
    T, D = x.shape
    N = w.shape[1]
    tm = min(tm, T)
    assert T % tm == 0 and N % tn == 0
    return pl.pallas_call(
        _proj_kernel,
        grid=(T // tm, N // tn),
        in_specs=[
            pl.BlockSpec((tm, D), lambda i, j: (i, 0)),
            pl.BlockSpec((1, D), lambda i, j: (0, 0)),
            pl.BlockSpec((D, tn), lambda i, j: (0, j)),
        ],
        out_specs=pl.BlockSpec((tm, tn), lambda i, j: (i, j)),
        out_shape=jax.ShapeDtypeStruct((T, N), BF16),
        scratch_shapes=[pltpu.VMEM((tm, D), BF16)],
        compiler_params=_params("parallel", "arbitrary"),
        name="in_proj",
    )(x, g.reshape(1, D), w)


def _cmp_kernel(t_ref, w1_ref, w2_ref, pe_ref, o_ref, xf_ref, c_ref):
    seq, dh = t_ref.shape
    nch, half = c_ref.shape
    xf_ref[...] = t_ref[...].astype(F32)
    for j in range(CMP_STRIDE):
        c_ref[:, j * dh:(j + 1) * dh] = xf_ref[pl.ds(j, nch, stride=CMP_STRIDE), :].astype(BF16)
    c = c_ref[...]
    a = _dot(c, w1_ref[:half, :])
    b = _dot(c, w1_ref[half:, :])
    pe = jnp.broadcast_to(pe_ref[...], (8, 2 * half))
    pe_term = _dot(pe, w1_ref[...])[0:1]
    pre = a + pltpu.roll(b, nch - 1, 0) + pe_term
    hid = (pre * jax.nn.sigmoid(pre)).astype(BF16)
    o_ref[...] = _dot(hid, w2_ref[...]).astype(o_ref.dtype)


def _compress(P, w1, w2, pe, lay, *, B, S):
    G, dh = N_KV_GROUPS, HEAD_DIM
    nch = S // CMP_STRIDE
    cw = CMP_STRIDE * dh
    hid = w1.shape[2]
    kc_blk, kv_stride = lay["kc"] // dh, (lay["vc"] - lay["kc"]) // dh
    return pl.pallas_call(
        _cmp_kernel,
        grid=(2, B, G),
        in_specs=[
            pl.BlockSpec((S, dh), lambda s, b, g: (b, kc_blk + s * kv_stride + g)),
            pl.BlockSpec((None, 2 * cw, hid), lambda s, b, g: (s, 0, 0)),
            pl.BlockSpec((None, hid, dh), lambda s, b, g: (s, 0, 0)),
            pl.BlockSpec((None, 1, 2 * cw), lambda s, b, g: (s, 0, 0)),
        ],
        out_specs=pl.BlockSpec((None, None, None, nch, dh), lambda s, b, g: (s, b, g, 0, 0)),
        out_shape=jax.ShapeDtypeStruct((2, B, G, nch, dh), BF16),
        scratch_shapes=[pltpu.VMEM((S, dh), F32), pltpu.VMEM((nch, cw), BF16)],
        compiler_params=_params("parallel", "parallel", "parallel"),
        name="compress",
    )(P, w1, w2, pe)


def _nsa_kernel(q_ref, gl_ref, kc_ref, vc_ref, ov_ref, ks_ref, vs_ref, kw_ref, vw_ref, o_ref,
                ka_ref, ve_ref, qa_ref, m_ref, acc_ref, s_ref, imp_ref, bt_ref, oc_ref, *, tq, tk, n_sel):
    hpg, dh = HEADS_PER_GROUP, HEAD_DIM
    s0 = pl.program_id(2) * tq
    t_col = s0 + lax.broadcasted_iota(jnp.int32, (tq, 1), 0)

    @pl.when(pl.program_id(2) == 0)
    def _():
        seq = ks_ref.shape[0]
        for c in range(seq // tk):
            r = slice(c * tk, (c + 1) * tk)
            kblk = lax.shift_right_logical(c * tk + lax.broadcasted_iota(jnp.int32, (tk, LANES), 0),
                                           SEL_BLOCK.bit_length() - 1)
            lane = lax.broadcasted_iota(jnp.int32, (tk, LANES), 1)
            ka_ref[r, 0:dh] = ks_ref[r, :]
            ka_ref[r, dh:2 * dh] = (kblk == lane).astype(BF16)
            ve_ref[r, 0:dh] = vs_ref[r, :]
            ve_ref[r, dh:2 * dh] = (lane == 0).astype(BF16)

    for h in range(hpg):
        qa_ref[h, :, 0:dh] = q_ref[:, h * dh:(h + 1) * dh]

    nk = kc_ref.shape[0]

    def compressed(ncol):
        n_idx = lax.broadcasted_iota(jnp.int32, (tq, ncol), 1)
        mask_c = (n_idx * CMP_STRIDE + (CMP_BLOCK - 1)) <= t_col
        p_sum = None
        for h in range(hpg):
            s_c = jnp.where(mask_c, _dot_nt(qa_ref[h, :, 0:dh], kc_ref[0:ncol, :]), -jnp.inf)
            m_c = jnp.max(s_c, axis=-1, keepdims=True)
            m_c = jnp.where(m_c == -jnp.inf, 0.0, m_c)
            p_c = jnp.exp2(s_c - m_c)
            p_c = p_c * (1.0 / jnp.maximum(jnp.sum(p_c, axis=-1, keepdims=True), 1e-30))
            oc_ref[h] = _dot(p_c.astype(BF16), vc_ref[0:ncol, :])
            p_sum = p_c if p_sum is None else p_sum + p_c
        p_hi = p_sum.astype(BF16)
        p_lo = (p_sum - p_hi.astype(F32)).astype(BF16)
        ov = ov_ref[0:ncol, :]
        imp_ref[...] = (_dot(p_hi, ov) + _dot(p_lo, ov)).T

    n_cols = (s0 + tq - CMP_BLOCK) // CMP_STRIDE + 1
    for c in range(LANES, nk + 1, LANES):
        pl.when((n_cols > c - LANES) & (n_cols <= c) if c > LANES else n_cols <= c)(
            functools.partial(compressed, c))

    bt_ref[...] = jnp.full_like(bt_ref, MASK_BIAS)

    def select_rows(nrows):
        blk = lax.broadcasted_iota(jnp.int32, (nrows, tq), 0)
        t_row = s0 + lax.broadcasted_iota(jnp.int32, (nrows, tq), 1)
        cur = lax.shift_right_logical(t_row, SEL_BLOCK.bit_length() - 1)
        forced = (blk == 0) | (blk == cur) | (blk == cur - 1)
        cand = jnp.where(forced | (blk * SEL_BLOCK > t_row), -FORCE_BONUS, imp_ref[0:nrows, :])
        ridx = blk.astype(F32)
        sc = cand
        for _ in range(n_sel - 3):
            mx = jnp.max(sc, axis=0, keepdims=True)
            idx = jnp.min(jnp.where(sc == mx, ridx, float(LANES)), axis=0, keepdims=True)
            sc = jnp.where(ridx == idx, -jnp.inf, sc)
        sel = forced | ((sc == -jnp.inf) & (cand > -1.0))
        bt_ref[0:nrows, :] = jnp.where(sel, 0.0, MASK_BIAS)

    n_live = (s0 + tq) // SEL_BLOCK
    lo, mid = LANES // 4, LANES // 2
    pl.when(n_live <= lo)(lambda: select_rows(lo))
    pl.when((n_live > lo) & (n_live <= mid))(lambda: select_rows(mid))
    pl.when(n_live > mid)(lambda: select_rows(LANES))
    bias = bt_ref[...].T.astype(BF16)
    for h in range(hpg):
        qa_ref[h, :, dh:2 * dh] = bias

    m_ref[...] = jnp.full_like(m_ref, MASK_BIAS)
    acc_ref[...] = jnp.zeros_like(acc_ref)

    def scores(j, h):
        return _dot_nt(qa_ref[h], ka_ref[pl.ds(pl.multiple_of(j * tk, tk), tk), :])

    def consume(s, j, h):
        m_prev = m_ref[h]
        m_new = jnp.maximum(m_prev, jnp.max(s, axis=-1, keepdims=True))
        alpha = jnp.exp2(m_prev - m_new)
        p = jnp.exp2(s - pltpu.repeat(m_new, tk // LANES, axis=1))
        pv = _dot(p.astype(BF16), ve_ref[pl.ds(pl.multiple_of(j * tk, tk), tk), :])
        acc_ref[h] = acc_ref[h] * pltpu.repeat(alpha, 2, axis=1) + pv
        m_ref[h] = m_new

    for h in range(hpg):
        s_ref[h] = scores(0, h)
    j_last = s0 // tk

    def body(j, carry):
        for h in range(hpg):
            s_next = scores(j + 1, h)
            consume(s_ref[h], j, h)
            s_ref[h] = s_next
        return carry

    lax.fori_loop(0, j_last, body, 0)
    causal = j_last * tk + lax.broadcasted_iota(jnp.int32, (tq, tk), 1) <= t_col
    for h in range(hpg):
        consume(jnp.where(causal, s_ref[h], MASK_BIAS), j_last, h)

    wk = WINDOW + tq
    start = pl.multiple_of(jnp.maximum(s0 - WINDOW, 0), tq)
    kpos = start + lax.broadcasted_iota(jnp.int32, (tq, wk), 1)
    mask_w = (kpos <= t_col) & (kpos > t_col - WINDOW)
    gates = jax.nn.sigmoid(gl_ref[...].astype(F32))
    for h in range(hpg):
        s_w = jnp.where(mask_w, _dot_nt(qa_ref[h, :, 0:dh], kw_ref[pl.ds(start, wk), :]), -jnp.inf)
        p_w = jnp.exp2(s_w - jnp.max(s_w, axis=-1, keepdims=True))
        r_w = 1.0 / jnp.sum(p_w, axis=-1, keepdims=True)
        o_w = _dot(p_w.astype(BF16), vw_ref[pl.ds(start, wk), :]) * r_w
        acc = acc_ref[h]
        o_s = acc[:, 0:dh] * (1.0 / acc[:, dh:dh + 1])
        out = (gates[:, h:h + 1] * oc_ref[h]
               + gates[:, hpg + h:hpg + h + 1] * o_s
               + gates[:, 2 * hpg + h:2 * hpg + h + 1] * o_w)
        o_ref[:, h * dh:(h + 1) * dh] = out.astype(o_ref.dtype)


def _nsa(P, cmp_kv, overlap, lay, *, B, S, tq=512, tk=512):
    G, dh, hpg = N_KV_GROUPS, HEAD_DIM, HEADS_PER_GROUP
    tq = min(tq, S)
    tk = min(tk, S)
    nk = cmp_kv.shape[3]
    n_sel = min(SEL_TOPK, S // SEL_BLOCK)
    assert S % tq == 0 and tk % tq == 0 and S % tk == 0 and S >= WINDOW + tq
    assert S // SEL_BLOCK <= LANES and nk % LANES == 0
    nq = S // tq
    qw = hpg * dh
    kern = functools.partial(_nsa_kernel, tq=tq, tk=tk, n_sel=n_sel)
    return pl.pallas_call(
        kern,
        grid=(B, G, nq),
        in_specs=[
            pl.BlockSpec((tq, qw), lambda b, g, i: (b * nq + i, lay["q"] // qw + g)),
            pl.BlockSpec((tq, LANES), lambda b, g, i: (b * nq + i, lay["gates"] // LANES + g)),
            pl.BlockSpec((None, None, None, nk, dh), lambda b, g, i: (0, b, g, 0, 0)),
            pl.BlockSpec((None, None, None, nk, dh), lambda b, g, i: (1, b, g, 0, 0)),
            pl.BlockSpec((nk, LANES), lambda b, g, i: (0, 0)),
            pl.BlockSpec((S, dh), lambda b, g, i: (b, lay["ks"] // dh + g)),
            pl.BlockSpec((S, dh), lambda b, g, i: (b, lay["vs"] // dh + g)),
            pl.BlockSpec((S, dh), lambda b, g, i: (b, lay["kw"] // dh + g)),
            pl.BlockSpec((S, dh), lambda b, g, i: (b, lay["vw"] // dh + g)),
        ],
        out_specs=pl.BlockSpec((tq, qw), lambda b, g, i: (b * nq + i, g)),
        out_shape=jax.ShapeDtypeStruct((B * S, G * qw), BF16),
        scratch_shapes=[
            pltpu.VMEM((S, 2 * dh), BF16),
            pltpu.VMEM((S, 2 * dh), BF16),
            pltpu.VMEM((hpg, tq, 2 * dh), BF16),
            pltpu.VMEM((hpg, tq, LANES), F32),
            pltpu.VMEM((hpg, tq, 2 * dh), F32),
            pltpu.VMEM((hpg, tq, tk), F32),
            pltpu.VMEM((LANES, tq), F32),
            pltpu.VMEM((LANES, tq), F32),
            pltpu.VMEM((hpg, tq, dh), F32),
        ],
        compiler_params=_params("parallel", "parallel", "arbitrary"),
        name="nsa",
    )(P, P, cmp_kv, cmp_kv, overlap, P, P, P, P)


def _mix_kernel(xp_ref, halo_ref, b_ref, gp_ref, gn_ref, pw_ref, ps_ref, wpu_ref, wnu_ref, o_ref, *, tm, S):
    halo_rows = halo_ref.shape[0]
    gd = POOL_GROUP_DIM
    t0 = (pl.program_id(0) * tm) % S
    x = xp_ref[...].astype(F32)
    halo = halo_ref[...].astype(F32) * (t0 > 0).astype(F32)
    xc = jnp.concatenate([halo, x], axis=0)
    tpos = t0 + lax.broadcasted_iota(jnp.int32, (tm, 1), 0)
    a_up = jnp.zeros((tm, o_ref.shape[1]), F32)
    for gi, win in enumerate(POOL_WINDOWS):
        cols = slice(gi * gd, (gi + 1) * gd)
        s = xc[:, cols]
        d = 1
        while d < win:
            s = s + pltpu.roll(s, d, 0)
            d *= 2
        cnt = jnp.minimum(tpos + 1, win).astype(F32)
        pooled = s[halo_rows:] / cnt - x[:, cols]
        y = _dot(pooled.astype(BF16), pw_ref[gi]) * ps_ref[:, cols]
        a_up = a_up + _dot(y.astype(BF16), wpu_ref[cols, :])
    b_up = _dot(b_ref[...], wnu_ref[...])
    merged = (jax.nn.sigmoid(gp_ref[...].astype(F32)) * a_up
              + jax.nn.sigmoid(gn_ref[...].astype(F32)) * b_up)
    o_ref[...] = merged.astype(o_ref.dtype)


def _mix(P, b_attn, pool_w, pool_scale, w_pool_up, w_nsa_up, lay, *, S, tm=256):
    T = P.shape[0]
    pw = pool_w.shape[0] * pool_w.shape[1]
    D = w_pool_up.shape[1]
    halo = max(POOL_WINDOWS)
    assert S % tm == 0 and tm % halo == 0 and lay["pool"] == 0
    kern = functools.partial(_mix_kernel, tm=tm, S=S)
    return pl.pallas_call(
        kern,
        grid=(T // tm,),
        in_specs=[
            pl.BlockSpec((tm, pw), lambda i: (i, 0)),
            pl.BlockSpec((halo, pw), lambda i: (jnp.maximum(i * (tm // halo) - 1, 0), 0)),
            pl.BlockSpec((tm, b_attn.shape[1]), lambda i: (i, 0)),
            pl.BlockSpec((tm, D), lambda i: (i, lay["g_pool"] // D)),
            pl.BlockSpec((tm, D), lambda i: (i, lay["g_nsa"] // D)),
            pl.BlockSpec(pool_w.shape, lambda i: (0, 0, 0)),
            pl.BlockSpec((1, pw), lambda i: (0, 0)),
            pl.BlockSpec(w_pool_up.shape, lambda i: (0, 0)),
            pl.BlockSpec(w_nsa_up.shape, lambda i: (0, 0)),
        ],
        out_specs=pl.BlockSpec((tm, D), lambda i: (i, 0)),
        out_shape=jax.ShapeDtypeStruct((T, D), BF16),
        compiler_params=_params("parallel"),
        name="mix",
    )(P, P, b_attn, P, P, pool_w, pool_scale.reshape(1, pw), w_pool_up, w_nsa_up)


def _outproj_kernel(h_ref, m_ref, w_ref, o_ref):
    o_ref[...] = h_ref[...] + _dot(m_ref[...], w_ref[...])


def _outproj(h, merged, w_out, *, tm=512):
    T, D = h.shape
    assert T % tm == 0
    return pl.pallas_call(
        _outproj_kernel,
        grid=(T // tm,),
        in_specs=[
            pl.BlockSpec((tm, D), lambda i: (i, 0)),
            pl.BlockSpec((tm, D), lambda i: (i, 0)),
            pl.BlockSpec(w_out.shape, lambda i: (0, 0)),
        ],
        out_specs=pl.BlockSpec((tm, D), lambda i: (i, 0)),
        out_shape=jax.ShapeDtypeStruct((T, D), F32),
        compiler_params=_params("parallel"),
        name="out_proj",
    )(h, merged, w_out)


def _prep_w_in(w_in, pool_width, d_model):
    G, hpg, dh = N_KV_GROUPS, HEADS_PER_GROUP, HEAD_DIM
    qw, kvw = N_HEADS * dh, G * dh
    o_q = pool_width
    o_kv = o_q + qw
    o_gate = o_kv + 6 * kvw
    o_gp = o_gate + N_NSA_BRANCHES * N_HEADS
    o_gn = o_gp + d_model
    assert w_in.shape[1] == o_gn + d_model
    gate_cols = w_in[:, o_gate:o_gp].reshape(-1, N_NSA_BRANCHES, G, hpg)
    gate_cols = jnp.transpose(gate_cols, (0, 2, 1, 3)).reshape(-1, G, N_NSA_BRANCHES * hpg)
    gate_cols = jnp.pad(gate_cols, ((0, 0), (0, 0), (0, LANES - N_NSA_BRANCHES * hpg))).reshape(-1, G * LANES)
    pieces = [
        w_in[:, :o_q],
        w_in[:, o_q:o_kv] * (LOG2E * dh ** -0.5),
        w_in[:, o_kv:o_gate],
        w_in[:, o_gp:o_gn],
        w_in[:, o_gn:],
        gate_cols,
    ]
    w = jnp.concatenate([p.astype(BF16) for p in pieces], axis=1)
    lay = {"pool": 0, "q": o_q, "kc": o_kv, "vc": o_kv + kvw, "ks": o_kv + 2 * kvw, "vs": o_kv + 3 * kvw,
           "kw": o_kv + 4 * kvw, "vw": o_kv + 5 * kvw, "g_pool": o_gate, "g_nsa": o_gate + d_model,
           "gates": o_gate + 2 * d_model}
    return w, lay


def _overlap_matrix(nk, n_cmp, n_blk):
    n = jnp.arange(nk)[:, None]
    m = jnp.arange(LANES)[None, :]
    cs = n * CMP_STRIDE
    ss = m * SEL_BLOCK
    ov = (cs <= ss + SEL_BLOCK - 1) & (cs + CMP_BLOCK - 1 >= ss) & (n < n_cmp) & (m < n_blk)
    return ov.astype(BF16)


def kernel(x, ffn1_norm, ffn1_w_gate, ffn1_w_up, ffn1_w_down, mix_norm, w_in, pool_w, pool_scale,
           cmp_pe_k, cmp_pe_v, cmp_k_w1, cmp_k_w2, cmp_v_w1, cmp_v_w2, w_pool_up, w_nsa_up, w_out,
           ffn2_norm, ffn2_w_gate, ffn2_w_up, ffn2_w_down, final_norm):
    B, S, D = x.shape
    depth = ffn1_norm.shape[0]
    G, dh = N_KV_GROUPS, HEAD_DIM
    T = B * S
    pool_width = pool_w.shape[1] * pool_w.shape[2]
    n_chunk = S // CMP_STRIDE
    n_cmp = n_chunk - CMP_BLOCK // CMP_STRIDE + 1
    n_blk = S // SEL_BLOCK
    cw = CMP_STRIDE * dh

    h = x.reshape(T, D)
    for l in range(depth):
        last = l == depth - 1
        h = _ffn(h, ffn1_norm[l], ffn1_w_gate[l].astype(BF16), ffn1_w_up[l].astype(BF16),
                 ffn1_w_down[l].astype(BF16))

        w_p, lay = _prep_w_in(w_in[l], pool_width, D)
        P = _proj(h, mix_norm[l], w_p)

        w1 = jnp.stack([cmp_k_w1[l], cmp_v_w1[l]]).astype(BF16)
        w2 = jnp.stack([cmp_k_w2[l], cmp_v_w2[l]]).astype(BF16)
        pe = jnp.stack([cmp_pe_k[l], cmp_pe_v[l]]).reshape(2, 1, CMP_BLOCK * dh).astype(BF16)
        cmp_kv = _compress(P, w1, w2, pe, lay, B=B, S=S)

        overlap = _overlap_matrix(n_chunk, n_cmp, n_blk)

        b_attn = _nsa(P, cmp_kv, overlap, lay, B=B, S=S)
        merged = _mix(P, b_attn, pool_w[l].astype(BF16), pool_scale[l], w_pool_up[l].astype(BF16),
                      w_nsa_up[l].astype(BF16), lay, S=S)
        h = _outproj(h, merged, w_out[l].astype(BF16))
        h = _ffn(h, ffn2_norm[l], ffn2_w_gate[l].astype(BF16), ffn2_w_up[l].astype(BF16),
                 ffn2_w_down[l].astype(BF16), final_norm if last else None)
    if depth == 0:
        raise ValueError("depth must be positive")
    return h.reshape(B, S, D)
```

```python
import functools

import jax
import jax.numpy as jnp
from jax import lax
from jax.experimental import pallas as pl
from jax.experimental.pallas import tpu as pltpu

F32 = jnp.float32
BF16 = jnp.bfloat16

POOL_WINDOWS = (2, 4, 8, 16)
POOL_GROUP_DIM = 256
N_HEADS = 16
N_KV_GROUPS = 4
HEADS_PER_GROUP = N_HEADS // N_KV_GROUPS
HEAD_DIM = 128
CMP_BLOCK = 32
CMP_STRIDE = 16
SEL_BLOCK = 64
SEL_TOPK = 16
WINDOW = 512
N_NSA_BRANCHES = 3
FORCE_BONUS = 1000.0
RMS_EPS = 1e-6

LOG2E = 1.4426950408889634
LANES = 128
MASK_BIAS = -1e30
VMEM_LIMIT = 56 * 1024 * 1024


def _dot(a, b):
    return jnp.dot(a, b, preferred_element_type=F32)


def _dot_nt(a, b):
    return lax.dot_general(a, b, (((1,), (1,)), ((), ())), preferred_element_type=F32)


def _rms(x, g):
    ms = jnp.mean(x * x, axis=-1, keepdims=True)
    return x * lax.rsqrt(ms + RMS_EPS) * g


def _params(*sem):
    return pltpu.CompilerParams(dimension_semantics=sem, vmem_limit_bytes=VMEM_LIMIT)


def _ffn_kernel(*refs, final):
    if final:
        x_ref, g_ref, wg_ref, wu_ref, wd_ref, fg_ref, o_ref, xn_ref, acc_ref = refs
    else:
        x_ref, g_ref, wg_ref, wu_ref, wd_ref, o_ref, xn_ref, acc_ref = refs
    j = pl.program_id(1)

    @pl.when(j == 0)
    def _():
        xn_ref[...] = _rms(x_ref[...], g_ref[...]).astype(BF16)
        acc_ref[...] = jnp.zeros_like(acc_ref)

    xn = xn_ref[...]
    half = wg_ref.shape[1] // 2
    for c in range(2):
        cols = slice(c * half, (c + 1) * half)
        gate = _dot(xn, wg_ref[:, cols])
        up = _dot(xn, wu_ref[:, cols])
        act = (gate * jax.nn.sigmoid(gate) * up).astype(BF16)
        acc_ref[...] += _dot(act, wd_ref[cols, :])

    @pl.when(j == pl.num_programs(1) - 1)
    def _():
        h = x_ref[...] + 0.5 * acc_ref[...]
        if final:
            h = _rms(h, fg_ref[...])
        o_ref[...] = h


def _ffn(x, g, wg, wu, wd, final_g=None, *, tm=512, tf=512):
    T, D = x.shape
    F = wg.shape[1]
    assert T % tm == 0 and F % tf == 0
    final = final_g is not None
    in_specs = [
        pl.BlockSpec((tm, D), lambda i, j: (i, 0)),
        pl.BlockSpec((1, D), lambda i, j: (0, 0)),
        pl.BlockSpec((D, tf), lambda i, j: (0, j)),
        pl.BlockSpec((D, tf), lambda i, j: (0, j)),
        pl.BlockSpec((tf, D), lambda i, j: (j, 0)),
    ]
    args = [x, g.reshape(1, D), wg, wu, wd]
    if final:
        in_specs.append(pl.BlockSpec((1, D), lambda i, j: (0, 0)))
        args.append(final_g.reshape(1, D))
    return pl.pallas_call(
        functools.partial(_ffn_kernel, final=final),
        grid=(T // tm, F // tf),
        in_specs=in_specs,
        out_specs=pl.BlockSpec((tm, D), lambda i, j: (i, 0)),
        out_shape=jax.ShapeDtypeStruct((T, D), F32),
        scratch_shapes=[pltpu.VMEM((tm, D), BF16), pltpu.VMEM((tm, D), F32)],
        compiler_params=_params("parallel", "arbitrary"),
        name="ffn_final" if final else "ffn",
    )(*args)


def _proj_kernel(x_ref, g_ref, w_ref, o_ref, xn_ref):
    @pl.when(pl.program_id(1) == 0)
    def _():
        xn_ref[...] = _rms(x_ref[...], g_ref[...]).astype(BF16)

    o_ref[...] = _dot(xn_ref[...], w_ref[...]).astype(o_ref.dtype)


def _proj(x, g, w, *, tm=1024, tn=2176):
    T, D = x.shape
    N = w.shape[1]
    tm = min(tm, T)
    assert T % tm == 0 and N % tn == 0
    return pl.pallas_call(
        _proj_kernel,
        grid=(T // tm, N // tn),
        in_specs=[
            pl.BlockSpec((tm, D), lambda i, j: (i, 0)),
            pl.BlockSpec((1, D), lambda i, j: (0, 0)),
            pl.BlockSpec((D, tn), lambda i, j: (0, j)),
        ],
        out_specs=pl.BlockSpec((tm, tn), lambda i, j: (i, j)),
        out_shape=jax.ShapeDtypeStruct((T, N), BF16),
        scratch_shapes=[pltpu.VMEM((tm, D), BF16)],
        compiler_params=_params("parallel", "arbitrary"),
        name="in_proj",
    )(x, g.reshape(1, D), w)


def _cmp_kernel(t_ref, w1_ref, w2_ref, pe_ref, o_ref, xf_ref, c_ref):
    seq, dh = t_ref.shape
    nch, half = c_ref.shape
    xf_ref[...] = t_ref[...].astype(F32)
    for j in range(CMP_STRIDE):
        c_ref[:, j * dh:(j + 1) * dh] = xf_ref[pl.ds(j, nch, stride=CMP_STRIDE), :].astype(BF16)
    c = c_ref[...]
    a = _dot(c, w1_ref[:half, :])
    b = _dot(c, w1_ref[half:, :])
    pe = jnp.broadcast_to(pe_ref[...], (8, 2 * half))
    pe_term = _dot(pe, w1_ref[...])[0:1]
    pre = a + pltpu.roll(b, nch - 1, 0) + pe_term
    hid = (pre * jax.nn.sigmoid(pre)).astype(BF16)
    o_ref[...] = _dot(hid, w2_ref[...]).astype(o_ref.dtype)


def _compress(P, w1, w2, pe, lay, *, B, S):
    G, dh = N_KV_GROUPS, HEAD_DIM
    nch = S // CMP_STRIDE
    cw = CMP_STRIDE * dh
    hid = w1.shape[2]
    kc_blk, kv_stride = lay["kc"] // dh, (lay["vc"] - lay["kc"]) // dh
    return pl.pallas_call(
        _cmp_kernel,
        grid=(2, B, G),
        in_specs=[
            pl.BlockSpec((S, dh), lambda s, b, g: (b, kc_blk + s * kv_stride + g)),
            pl.BlockSpec((None, 2 * cw, hid), lambda s, b, g: (s, 0, 0)),
            pl.BlockSpec((None, hid, dh), lambda s, b, g: (s, 0, 0)),
            pl.BlockSpec((None, 1, 2 * cw), lambda s, b, g: (s, 0, 0)),
        ],
        out_specs=pl.BlockSpec((None, None, None, nch, dh), lambda s, b, g: (s, b, g, 0, 0)),
        out_shape=jax.ShapeDtypeStruct((2, B, G, nch, dh), BF16),
        scratch_shapes=[pltpu.VMEM((S, dh), F32), pltpu.VMEM((nch, cw), BF16)],
        compiler_params=_params("parallel", "parallel", "parallel"),
        name="compress",
    )(P, w1, w2, pe)


def _nsa_kernel(q_ref, gl_ref, kc_ref, vc_ref, ov_ref, ks_ref, vs_ref, kw_ref, vw_ref, o_ref,
                ka_ref, ve_ref, qa_ref, m_ref, acc_ref, s_ref, imp_ref, bt_ref, oc_ref, *, tq, tk, n_sel):
    hpg, dh = HEADS_PER_GROUP, HEAD_DIM
    s0 = pl.program_id(2) * tq
    t_col = s0 + lax.broadcasted_iota(jnp.int32, (tq, 1), 0)

    @pl.when(pl.program_id(2) == 0)
    def _():
        seq = ks_ref.shape[0]
        for c in range(seq // tk):
            r = slice(c * tk, (c + 1) * tk)
            kblk = lax.shift_right_logical(c * tk + lax.broadcasted_iota(jnp.int32, (tk, LANES), 0),
                                           SEL_BLOCK.bit_length() - 1)
            lane = lax.broadcasted_iota(jnp.int32, (tk, LANES), 1)
            ka_ref[r, 0:dh] = ks_ref[r, :]
            ka_ref[r, dh:2 * dh] = (kblk == lane).astype(BF16)
            ve_ref[r, 0:dh] = vs_ref[r, :]
            ve_ref[r, dh:2 * dh] = (lane == 0).astype(BF16)

    for h in range(hpg):
        qa_ref[h, :, 0:dh] = q_ref[:, h * dh:(h + 1) * dh]

    nk = kc_ref.shape[0]

    def compressed(ncol):
        n_idx = lax.broadcasted_iota(jnp.int32, (tq, ncol), 1)
        mask_c = (n_idx * CMP_STRIDE + (CMP_BLOCK - 1)) <= t_col
        p_sum = None
        for h in range(hpg):
            s_c = jnp.where(mask_c, _dot_nt(qa_ref[h, :, 0:dh], kc_ref[0:ncol, :]), -jnp.inf)
            m_c = jnp.max(s_c, axis=-1, keepdims=True)
            m_c = jnp.where(m_c == -jnp.inf, 0.0, m_c)
            p_c = jnp.exp2(s_c - m_c)
            p_c = p_c * (1.0 / jnp.maximum(jnp.sum(p_c, axis=-1, keepdims=True), 1e-30))
            oc_ref[h] = _dot(p_c.astype(BF16), vc_ref[0:ncol, :])
            p_sum = p_c if p_sum is None else p_sum + p_c
        p_hi = p_sum.astype(BF16)
        p_lo = (p_sum - p_hi.astype(F32)).astype(BF16)
        ov = ov_ref[0:ncol, :]
        imp_ref[...] = (_dot(p_hi, ov) + _dot(p_lo, ov)).T

    n_cols = (s0 + tq - CMP_BLOCK) // CMP_STRIDE + 1
    for c in range(LANES, nk + 1, LANES):
        pl.when((n_cols > c - LANES) & (n_cols <= c) if c > LANES else n_cols <= c)(
            functools.partial(compressed, c))

    bt_ref[...] = jnp.full_like(bt_ref, MASK_BIAS)

    def select_rows(nrows):
        blk = lax.broadcasted_iota(jnp.int32, (nrows, tq), 0)
        t_row = s0 + lax.broadcasted_iota(jnp.int32, (nrows, tq), 1)
        cur = lax.shift_right_logical(t_row, SEL_BLOCK.bit_length() - 1)
        forced = (blk == 0) | (blk == cur) | (blk == cur - 1)
        cand = jnp.where(forced | (blk * SEL_BLOCK > t_row), -FORCE_BONUS, imp_ref[0:nrows, :])
        ridx = blk.astype(F32)
        sc = cand
        for _ in range(n_sel - 3):
            mx = jnp.max(sc, axis=0, keepdims=True)
            idx = jnp.min(jnp.where(sc == mx, ridx, float(LANES)), axis=0, keepdims=True)
            sc = jnp.where(ridx == idx, -jnp.inf, sc)
        sel = forced | ((sc == -jnp.inf) & (cand > -1.0))
        bt_ref[0:nrows, :] = jnp.where(sel, 0.0, MASK_BIAS)

    n_live = (s0 + tq) // SEL_BLOCK
    lo, mid = LANES // 4, LANES // 2
    pl.when(n_live <= lo)(lambda: select_rows(lo))
    pl.when((n_live > lo) & (n_live <= mid))(lambda: select_rows(mid))
    pl.when(n_live > mid)(lambda: select_rows(LANES))
    bias = bt_ref[...].T.astype(BF16)
    for h in range(hpg):
        qa_ref[h, :, dh:2 * dh] = bias

    m_ref[...] = jnp.full_like(m_ref, MASK_BIAS)
    acc_ref[...] = jnp.zeros_like(acc_ref)

    def scores(j, h):
        return _dot_nt(qa_ref[h], ka_ref[pl.ds(pl.multiple_of(j * tk, tk), tk), :])

    def consume(s, j, h):
        m_prev = m_ref[h]
        m_new = jnp.maximum(m_prev, jnp.max(s, axis=-1, keepdims=True))
        alpha = jnp.exp2(m_prev - m_new)
        p = jnp.exp2(s - pltpu.repeat(m_new, tk // LANES, axis=1))
        pv = _dot(p.astype(BF16), ve_ref[pl.ds(pl.multiple_of(j * tk, tk), tk), :])
        acc_ref[h] = acc_ref[h] * pltpu.repeat(alpha, 2, axis=1) + pv
        m_ref[h] = m_new

    for h in range(hpg):
        s_ref[h] = scores(0, h)
    j_last = s0 // tk

    def body(j, carry):
        for h in range(hpg):
            s_next = scores(j + 1, h)
            consume(s_ref[h], j, h)
            s_ref[h] = s_next
        return carry

    lax.fori_loop(0, j_last, body, 0)
    causal = j_last * tk + lax.broadcasted_iota(jnp.int32, (tq, tk), 1) <= t_col
    for h in range(hpg):
        consume(jnp.where(causal, s_ref[h], MASK_BIAS), j_last, h)

    wk = WINDOW + tq
    start = pl.multiple_of(jnp.maximum(s0 - WINDOW, 0), tq)
    kpos = start + lax.broadcasted_iota(jnp.int32, (tq, wk), 1)
    mask_w = (kpos <= t_col) & (kpos > t_col - WINDOW)
    gates = jax.nn.sigmoid(gl_ref[...].astype(F32))
    for h in range(hpg):
        s_w = jnp.where(mask_w, _dot_nt(qa_ref[h, :, 0:dh], kw_ref[pl.ds(start, wk), :]), -jnp.inf)
        p_w = jnp.exp2(s_w - jnp.max(s_w, axis=-1, keepdims=True))
        r_w = 1.0 / jnp.sum(p_w, axis=-1, keepdims=True)
        o_w = _dot(p_w.astype(BF16), vw_ref[pl.ds(start, wk), :]) * r_w
        acc = acc_ref[h]
        o_s = acc[:, 0:dh] * (1.0 / acc[:, dh:dh + 1])
        out = (gates[:, h:h + 1] * oc_ref[h]
               + gates[:, hpg + h:hpg + h + 1] * o_s
               + gates[:, 2 * hpg + h:2 * hpg + h + 1] * o_w)
        o_ref[:, h * dh:(h + 1) * dh] = out.astype(o_ref.dtype)


def _nsa(P, cmp_kv, overlap, lay, *, B, S, tq=512, tk=512):
    G, dh, hpg = N_KV_GROUPS, HEAD_DIM, HEADS_PER_GROUP
    tq = min(tq, S)
    tk = min(tk, S)
    nk = cmp_kv.shape[3]
    n_sel = min(SEL_TOPK, S // SEL_BLOCK)
    assert S % tq == 0 and tk % tq == 0 and S % tk == 0 and S >= WINDOW + tq
    assert S // SEL_BLOCK <= LANES and nk % LANES == 0
    nq = S // tq
    qw = hpg * dh
    kern = functools.partial(_nsa_kernel, tq=tq, tk=tk, n_sel=n_sel)
    return pl.pallas_call(
        kern,
        grid=(B, G, nq),
        in_specs=[
            pl.BlockSpec((tq, qw), lambda b, g, i: (b * nq + i, lay["q"] // qw + g)),
            pl.BlockSpec((tq, LANES), lambda b, g, i: (b * nq + i, lay["gates"] // LANES + g)),
            pl.BlockSpec((None, None, None, nk, dh), lambda b, g, i: (0, b, g, 0, 0)),
            pl.BlockSpec((None, None, None, nk, dh), lambda b, g, i: (1, b, g, 0, 0)),
            pl.BlockSpec((nk, LANES), lambda b, g, i: (0, 0)),
            pl.BlockSpec((S, dh), lambda b, g, i: (b, lay["ks"] // dh + g)),
            pl.BlockSpec((S, dh), lambda b, g, i: (b, lay["vs"] // dh + g)),
            pl.BlockSpec((S, dh), lambda b, g, i: (b, lay["kw"] // dh + g)),
            pl.BlockSpec((S, dh), lambda b, g, i: (b, lay["vw"] // dh + g)),
        ],
        out_specs=pl.BlockSpec((tq, qw), lambda b, g, i: (b * nq + i, g)),
        out_shape=jax.ShapeDtypeStruct((B * S, G * qw), BF16),
        scratch_shapes=[
            pltpu.VMEM((S, 2 * dh), BF16),
            pltpu.VMEM((S, 2 * dh), BF16),
            pltpu.VMEM((hpg, tq, 2 * dh), BF16),
            pltpu.VMEM((hpg, tq, LANES), F32),
            pltpu.VMEM((hpg, tq, 2 * dh), F32),
            pltpu.VMEM((hpg, tq, tk), F32),
            pltpu.VMEM((LANES, tq), F32),
            pltpu.VMEM((LANES, tq), F32),
            pltpu.VMEM((hpg, tq, dh), F32),
        ],
        compiler_params=_params("parallel", "parallel", "arbitrary"),
        name="nsa",
    )(P, P, cmp_kv, cmp_kv, overlap, P, P, P, P)


def _shifted_cols(a, nxt, shift):
    cat = jnp.concatenate([a, nxt], axis=1).astype(F32)
    return pltpu.roll(cat, cat.shape[1] - shift, 1)[:, :a.shape[1]]


def _mix_kernel(xp_ref, halo_ref, b_ref, ga_ref, gb_ref, gc_ref, pw_ref, ps_ref, wpu_ref, wnu_ref, o_ref, *, tm, S, shift):
    halo_rows = halo_ref.shape[0]
    gd = POOL_GROUP_DIM
    t0 = (pl.program_id(0) * tm) % S
    x = xp_ref[...].astype(F32)
    halo = halo_ref[...].astype(F32) * (t0 > 0).astype(F32)
    xc = jnp.concatenate([halo, x], axis=0)
    tpos = t0 + lax.broadcasted_iota(jnp.int32, (tm, 1), 0)
    a_up = jnp.zeros((tm, o_ref.shape[1]), F32)
    for gi, win in enumerate(POOL_WINDOWS):
        cols = slice(gi * gd, (gi + 1) * gd)
        s = xc[:, cols]
        d = 1
        while d < win:
            s = s + pltpu.roll(s, d, 0)
            d *= 2
        cnt = jnp.minimum(tpos + 1, win).astype(F32)
        pooled = s[halo_rows:] / cnt - x[:, cols]
        y = _dot(pooled.astype(BF16), pw_ref[gi]) * ps_ref[:, cols]
        a_up = a_up + _dot(y.astype(BF16), wpu_ref[cols, :])
    b_up = _dot(b_ref[...], wnu_ref[...])
    g_pool = _shifted_cols(ga_ref[...], gb_ref[:, 0:LANES], shift)
    g_nsa = _shifted_cols(gb_ref[...], gc_ref[...], shift)
    merged = jax.nn.sigmoid(g_pool) * a_up + jax.nn.sigmoid(g_nsa) * b_up
    o_ref[...] = merged.astype(o_ref.dtype)


def _mix(P, b_attn, pool_w, pool_scale, w_pool_up, w_nsa_up, lay, *, S, tm=256):
    T = P.shape[0]
    pw = pool_w.shape[0] * pool_w.shape[1]
    D = w_pool_up.shape[1]
    halo = max(POOL_WINDOWS)
    assert S % tm == 0 and tm % halo == 0 and lay["pool"] == 0
    kern = functools.partial(_mix_kernel, tm=tm, S=S, shift=lay["tail_shift"])
    return pl.pallas_call(
        kern,
        grid=(T // tm,),
        in_specs=[
            pl.BlockSpec((tm, pw), lambda i: (i, 0)),
            pl.BlockSpec((halo, pw), lambda i: (jnp.maximum(i * (tm // halo) - 1, 0), 0)),
            pl.BlockSpec((tm, b_attn.shape[1]), lambda i: (i, 0)),
            pl.BlockSpec((tm, D), lambda i: (i, lay["tail"] // D)),
            pl.BlockSpec((tm, D), lambda i: (i, lay["tail"] // D + 1)),
            pl.BlockSpec((tm, LANES), lambda i: (i, (lay["tail"] + 2 * D) // LANES)),
            pl.BlockSpec(pool_w.shape, lambda i: (0, 0, 0)),
            pl.BlockSpec((1, pw), lambda i: (0, 0)),
            pl.BlockSpec(w_pool_up.shape, lambda i: (0, 0)),
            pl.BlockSpec(w_nsa_up.shape, lambda i: (0, 0)),
        ],
        out_specs=pl.BlockSpec((tm, D), lambda i: (i, 0)),
        out_shape=jax.ShapeDtypeStruct((T, D), BF16),
        compiler_params=_params("parallel"),
        name="mix",
    )(P, P, b_attn, P, P, P, pool_w, pool_scale.reshape(1, pw), w_pool_up, w_nsa_up)


def _outproj_kernel(h_ref, m_ref, w_ref, o_ref):
    o_ref[...] = h_ref[...] + _dot(m_ref[...], w_ref[...])


def _outproj(h, merged, w_out, *, tm=512):
    T, D = h.shape
    assert T % tm == 0
    return pl.pallas_call(
        _outproj_kernel,
        grid=(T // tm,),
        in_specs=[
            pl.BlockSpec((tm, D), lambda i: (i, 0)),
            pl.BlockSpec((tm, D), lambda i: (i, 0)),
            pl.BlockSpec(w_out.shape, lambda i: (0, 0)),
        ],
        out_specs=pl.BlockSpec((tm, D), lambda i: (i, 0)),
        out_shape=jax.ShapeDtypeStruct((T, D), F32),
        compiler_params=_params("parallel"),
        name="out_proj",
    )(h, merged, w_out)


def _prep_w_in(w_in, pool_width, d_model):
    G, hpg, dh = N_KV_GROUPS, HEADS_PER_GROUP, HEAD_DIM
    qw, kvw = N_HEADS * dh, G * dh
    o_q = pool_width
    o_kv = o_q + qw
    o_gate = o_kv + 6 * kvw
    o_gp = o_gate + N_NSA_BRANCHES * N_HEADS
    o_gn = o_gp + d_model
    assert w_in.shape[1] == o_gn + d_model
    gate_cols = w_in[:, o_gate:o_gp].reshape(-1, N_NSA_BRANCHES, G, hpg)
    gate_cols = jnp.transpose(gate_cols, (0, 2, 1, 3)).reshape(-1, G, N_NSA_BRANCHES * hpg)
    gate_cols = jnp.pad(gate_cols, ((0, 0), (0, 0), (0, LANES - N_NSA_BRANCHES * hpg))).reshape(-1, G * LANES)
    tail = w_in[:, o_gate:]
    tail_pad = (-tail.shape[1]) % LANES
    pieces = [
        w_in[:, :o_q],
        w_in[:, o_q:o_kv] * (LOG2E * dh ** -0.5),
        w_in[:, o_kv:o_gate],
        tail,
        jnp.zeros((w_in.shape[0], tail_pad), w_in.dtype),
        gate_cols,
    ]
    w = jnp.concatenate([p.astype(BF16) for p in pieces], axis=1)
    assert o_gate % d_model == 0 and o_gp - o_gate < LANES
    lay = {"pool": 0, "q": o_q, "kc": o_kv, "vc": o_kv + kvw, "ks": o_kv + 2 * kvw, "vs": o_kv + 3 * kvw,
           "kw": o_kv + 4 * kvw, "vw": o_kv + 5 * kvw, "tail": o_gate, "tail_shift": o_gp - o_gate,
           "gates": o_gate + tail.shape[1] + tail_pad}
    return w, lay


def _overlap_matrix(nk, n_cmp, n_blk):
    n = jnp.arange(nk)[:, None]
    m = jnp.arange(LANES)[None, :]
    cs = n * CMP_STRIDE
    ss = m * SEL_BLOCK
    ov = (cs <= ss + SEL_BLOCK - 1) & (cs + CMP_BLOCK - 1 >= ss) & (n < n_cmp) & (m < n_blk)
    return ov.astype(BF16)


def kernel(x, ffn1_norm, ffn1_w_gate, ffn1_w_up, ffn1_w_down, mix_norm, w_in, pool_w, pool_scale,
           cmp_pe_k, cmp_pe_v, cmp_k_w1, cmp_k_w2, cmp_v_w1, cmp_v_w2, w_pool_up, w_nsa_up, w_out,
           ffn2_norm, ffn2_w_gate, ffn2_w_up, ffn2_w_down, final_norm):
    B, S, D = x.shape
    depth = ffn1_norm.shape[0]
    G, dh = N_KV_GROUPS, HEAD_DIM
    T = B * S
    pool_width = pool_w.shape[1] * pool_w.shape[2]
    n_chunk = S // CMP_STRIDE
    n_cmp = n_chunk - CMP_BLOCK // CMP_STRIDE + 1
    n_blk = S // SEL_BLOCK
    cw = CMP_STRIDE * dh

    h = x.reshape(T, D)
    for l in range(depth):
        last = l == depth - 1
        h = _ffn(h, ffn1_norm[l], ffn1_w_gate[l].astype(BF16), ffn1_w_up[l].astype(BF16),
                 ffn1_w_down[l].astype(BF16))

        w_p, lay = _prep_w_in(w_in[l], pool_width, D)
        P = _proj(h, mix_norm[l], w_p)

        w1 = jnp.stack([cmp_k_w1[l], cmp_v_w1[l]]).astype(BF16)
        w2 = jnp.stack([cmp_k_w2[l], cmp_v_w2[l]]).astype(BF16)
        pe = jnp.stack([cmp_pe_k[l], cmp_pe_v[l]]).reshape(2, 1, CMP_BLOCK * dh).astype(BF16)
        cmp_kv = _compress(P, w1, w2, pe, lay, B=B, S=S)

        overlap = _overlap_matrix(n_chunk, n_cmp, n_blk)

        b_attn = _nsa(P, cmp_kv, overlap, lay, B=B, S=S)
        merged = _mix(P, b_attn, pool_w[l].astype(BF16), pool_scale[l], w_pool_up[l].astype(BF16),
                      w_nsa_up[l].astype(BF16), lay, S=S)
        h = _outproj(h, merged, w_out[l].astype(BF16))
        h = _ffn(h, ffn2_norm[l], ffn2_w_gate[l].astype(BF16), ffn2_w_up[l].astype(BF16),
                 ffn2_w_down[l].astype(BF16), final_norm if last else None)
    if depth == 0:
        raise ValueError("depth must be positive")
    return h.reshape(B, S, D)
```

```python
import functools

import jax
import jax.numpy as jnp
from jax import lax
from jax.experimental import pallas as pl
from jax.experimental.pallas import tpu as pltpu

F32 = jnp.float32
BF16 = jnp.bfloat16

POOL_WINDOWS = (2, 4, 8, 16)
POOL_GROUP_DIM = 256
N_HEADS = 16
N_KV_GROUPS = 4
HEADS_PER_GROUP = N_HEADS // N_KV_GROUPS
HEAD_DIM = 128
CMP_BLOCK = 32
CMP_STRIDE = 16
SEL_BLOCK = 64
SEL_TOPK = 16
WINDOW = 512
N_NSA_BRANCHES = 3
FORCE_BONUS = 1000.0
RMS_EPS = 1e-6

LOG2E = 1.4426950408889634
LANES = 128
MASK_BIAS = -1e30
VMEM_LIMIT = 56 * 1024 * 1024


def _dot(a, b):
    return jnp.dot(a, b, preferred_element_type=F32)


def _dot_nt(a, b):
    return lax.dot_general(a, b, (((1,), (1,)), ((), ())), preferred_element_type=F32)


def _rms(x, g):
    ms = jnp.mean(x * x, axis=-1, keepdims=True)
    return x * lax.rsqrt(ms + RMS_EPS) * g


def _params(*sem):
    return pltpu.CompilerParams(dimension_semantics=sem, vmem_limit_bytes=VMEM_LIMIT)


def _ffn_kernel(*refs, final):
    if final:
        x_ref, g_ref, wg_ref, wu_ref, wd_ref, fg_ref, o_ref, xn_ref, acc_ref = refs
    else:
        x_ref, g_ref, wg_ref, wu_ref, wd_ref, o_ref, xn_ref, acc_ref = refs
    j = pl.program_id(1)

    @pl.when(j == 0)
    def _():
        xn_ref[...] = _rms(x_ref[...], g_ref[...]).astype(BF16)
        acc_ref[...] = jnp.zeros_like(acc_ref)

    xn = xn_ref[...]
    half = wg_ref.shape[1] // 2
    for c in range(2):
        cols = slice(c * half, (c + 1) * half)
        gate = _dot(xn, wg_ref[:, cols])
        up = _dot(xn, wu_ref[:, cols])
        act = (gate * jax.nn.sigmoid(gate) * up).astype(BF16)
        acc_ref[...] += _dot(act, wd_ref[cols, :])

    @pl.when(j == pl.num_programs(1) - 1)
    def _():
        h = x_ref[...] + 0.5 * acc_ref[...]
        if final:
            h = _rms(h, fg_ref[...])
        o_ref[...] = h


def _ffn(x, g, wg, wu, wd, final_g=None, *, tm=512, tf=512):
    T, D = x.shape
    F = wg.shape[1]
    assert T % tm == 0 and F % tf == 0
    final = final_g is not None
    in_specs = [
        pl.BlockSpec((tm, D), lambda i, j: (i, 0)),
        pl.BlockSpec((1, D), lambda i, j: (0, 0)),
        pl.BlockSpec((D, tf), lambda i, j: (0, j)),
        pl.BlockSpec((D, tf), lambda i, j: (0, j)),
        pl.BlockSpec((tf, D), lambda i, j: (j, 0)),
    ]
    args = [x, g.reshape(1, D), wg, wu, wd]
    if final:
        in_specs.append(pl.BlockSpec((1, D), lambda i, j: (0, 0)))
        args.append(final_g.reshape(1, D))
    return pl.pallas_call(
        functools.partial(_ffn_kernel, final=final),
        grid=(T // tm, F // tf),
        in_specs=in_specs,
        out_specs=pl.BlockSpec((tm, D), lambda i, j: (i, 0)),
        out_shape=jax.ShapeDtypeStruct((T, D), F32),
        scratch_shapes=[pltpu.VMEM((tm, D), BF16), pltpu.VMEM((tm, D), F32)],
        compiler_params=_params("parallel", "arbitrary"),
        name="ffn_final" if final else "ffn",
    )(*args)


def _proj_kernel(x_ref, g_ref, w_ref, o_ref, xn_ref):
    @pl.when(pl.program_id(1) == 0)
    def _():
        xn_ref[...] = _rms(x_ref[...], g_ref[...]).astype(BF16)

    xn = xn_ref[...]
    half = w_ref.shape[1] // 2
    for c in range(2):
        cols = slice(c * half, (c + 1) * half)
        o_ref[:, cols] = _dot(xn, w_ref[:, cols]).astype(o_ref.dtype)


def _proj(x, g, w, *, tm=1024, tn=1536):
    T, D = x.shape
    N = w.shape[1]
    tm = min(tm, T)
    assert T % tm == 0 and N % tn == 0
    return pl.pallas_call(
        _proj_kernel,
        grid=(T // tm, N // tn),
        in_specs=[
            pl.BlockSpec((tm, D), lambda i, j: (i, 0)),
            pl.BlockSpec((1, D), lambda i, j: (0, 0)),
            pl.BlockSpec((D, tn), lambda i, j: (0, j)),
        ],
        out_specs=pl.BlockSpec((tm, tn), lambda i, j: (i, j)),
        out_shape=jax.ShapeDtypeStruct((T, N), BF16),
        scratch_shapes=[pltpu.VMEM((tm, D), BF16)],
        compiler_params=_params("parallel", "arbitrary"),
        name="in_proj",
    )(x, g.reshape(1, D), w)


def _cmp_kernel(t_ref, w1_ref, w2_ref, pe_ref, o_ref, xf_ref, c_ref):
    seq, dh = t_ref.shape
    nch, half = c_ref.shape
    xf_ref[...] = t_ref[...].astype(F32)
    for j in range(CMP_STRIDE):
        c_ref[:, j * dh:(j + 1) * dh] = xf_ref[pl.ds(j, nch, stride=CMP_STRIDE), :].astype(BF16)
    c = c_ref[...]
    a = _dot(c, w1_ref[:half, :])
    b = _dot(c, w1_ref[half:, :])
    pe = jnp.broadcast_to(pe_ref[...], (8, 2 * half))
    pe_term = _dot(pe, w1_ref[...])[0:1]
    pre = a + pltpu.roll(b, nch - 1, 0) + pe_term
    hid = (pre * jax.nn.sigmoid(pre)).astype(BF16)
    o_ref[...] = _dot(hid, w2_ref[...]).astype(o_ref.dtype)


def _compress(P, w1, w2, pe, lay, *, B, S):
    G, dh = N_KV_GROUPS, HEAD_DIM
    nch = S // CMP_STRIDE
    cw = CMP_STRIDE * dh
    hid = w1.shape[2]
    kc_blk, kv_stride = lay["kc"] // dh, (lay["vc"] - lay["kc"]) // dh
    return pl.pallas_call(
        _cmp_kernel,
        grid=(2, B, G),
        in_specs=[
            pl.BlockSpec((S, dh), lambda s, b, g: (b, kc_blk + s * kv_stride + g)),
            pl.BlockSpec((None, 2 * cw, hid), lambda s, b, g: (s, 0, 0)),
            pl.BlockSpec((None, hid, dh), lambda s, b, g: (s, 0, 0)),
            pl.BlockSpec((None, 1, 2 * cw), lambda s, b, g: (s, 0, 0)),
        ],
        out_specs=pl.BlockSpec((None, None, None, nch, dh), lambda s, b, g: (s, b, g, 0, 0)),
        out_shape=jax.ShapeDtypeStruct((2, B, G, nch, dh), BF16),
        scratch_shapes=[pltpu.VMEM((S, dh), F32), pltpu.VMEM((nch, cw), BF16)],
        compiler_params=_params("parallel", "parallel", "parallel"),
        name="compress",
    )(P, w1, w2, pe)


def _nsa_kernel(q_ref, gl_ref, kc_ref, vc_ref, ov_ref, ks_ref, vs_ref, kw_ref, vw_ref, o_ref,
                ka_ref, ve_ref, qa_ref, m_ref, acc_ref, s_ref, imp_ref, bt_ref, oc_ref, *, tq, tk, n_sel):
    hpg, dh = HEADS_PER_GROUP, HEAD_DIM
    s0 = pl.program_id(2) * tq
    t_col = s0 + lax.broadcasted_iota(jnp.int32, (tq, 1), 0)

    @pl.when(pl.program_id(2) == 0)
    def _():
        seq = ks_ref.shape[0]
        for c in range(seq // tk):
            r = slice(c * tk, (c + 1) * tk)
            kblk = lax.shift_right_logical(c * tk + lax.broadcasted_iota(jnp.int32, (tk, LANES), 0),
                                           SEL_BLOCK.bit_length() - 1)
            lane = lax.broadcasted_iota(jnp.int32, (tk, LANES), 1)
            ka_ref[r, 0:dh] = ks_ref[r, :]
            ka_ref[r, dh:2 * dh] = (kblk == lane).astype(BF16)
            ve_ref[r, 0:dh] = vs_ref[r, :]
            ve_ref[r, dh:2 * dh] = (lane == 0).astype(BF16)

    for h in range(hpg):
        qa_ref[h, :, 0:dh] = q_ref[:, h * dh:(h + 1) * dh]

    nk = kc_ref.shape[0]

    def compressed(ncol):
        n_idx = lax.broadcasted_iota(jnp.int32, (tq, ncol), 1)
        mask_c = (n_idx * CMP_STRIDE + (CMP_BLOCK - 1)) <= t_col
        p_sum = None
        for h in range(hpg):
            s_c = jnp.where(mask_c, _dot_nt(qa_ref[h, :, 0:dh], kc_ref[0:ncol, :]), -jnp.inf)
            m_c = jnp.max(s_c, axis=-1, keepdims=True)
            m_c = jnp.where(m_c == -jnp.inf, 0.0, m_c)
            p_c = jnp.exp2(s_c - m_c)
            p_c = p_c * (1.0 / jnp.maximum(jnp.sum(p_c, axis=-1, keepdims=True), 1e-30))
            oc_ref[h] = _dot(p_c.astype(BF16), vc_ref[0:ncol, :])
            p_sum = p_c if p_sum is None else p_sum + p_c
        p_hi = p_sum.astype(BF16)
        p_lo = (p_sum - p_hi.astype(F32)).astype(BF16)
        ov = ov_ref[0:ncol, :]
        imp_ref[...] = (_dot(p_hi, ov) + _dot(p_lo, ov)).T

    n_cols = (s0 + tq - CMP_BLOCK) // CMP_STRIDE + 1
    for c in range(LANES, nk + 1, LANES):
        pl.when((n_cols > c - LANES) & (n_cols <= c) if c > LANES else n_cols <= c)(
            functools.partial(compressed, c))

    bt_ref[...] = jnp.full_like(bt_ref, MASK_BIAS)

    def select_rows(nrows):
        blk = lax.broadcasted_iota(jnp.int32, (nrows, tq), 0)
        t_row = s0 + lax.broadcasted_iota(jnp.int32, (nrows, tq), 1)
        cur = lax.shift_right_logical(t_row, SEL_BLOCK.bit_length() - 1)
        forced = (blk == 0) | (blk == cur) | (blk == cur - 1)
        cand = jnp.where(forced | (blk * SEL_BLOCK > t_row), -FORCE_BONUS, imp_ref[0:nrows, :])
        ridx = blk.astype(F32)
        sc = cand
        for _ in range(n_sel - 3):
            mx = jnp.max(sc, axis=0, keepdims=True)
            idx = jnp.min(jnp.where(sc == mx, ridx, float(LANES)), axis=0, keepdims=True)
            sc = jnp.where(ridx == idx, -jnp.inf, sc)
        sel = forced | ((sc == -jnp.inf) & (cand > -1.0))
        bt_ref[0:nrows, :] = jnp.where(sel, 0.0, MASK_BIAS)

    n_live = (s0 + tq) // SEL_BLOCK
    lo, mid = LANES // 4, LANES // 2
    pl.when(n_live <= lo)(lambda: select_rows(lo))
    pl.when((n_live > lo) & (n_live <= mid))(lambda: select_rows(mid))
    pl.when(n_live > mid)(lambda: select_rows(LANES))
    bias = bt_ref[...].T.astype(BF16)
    for h in range(hpg):
        qa_ref[h, :, dh:2 * dh] = bias

    m_ref[...] = jnp.full_like(m_ref, MASK_BIAS)
    acc_ref[...] = jnp.zeros_like(acc_ref)

    def scores(j, h):
        return _dot_nt(qa_ref[h], ka_ref[pl.ds(pl.multiple_of(j * tk, tk), tk), :])

    def consume(s, j, h):
        m_prev = m_ref[h]
        m_new = jnp.maximum(m_prev, jnp.max(s, axis=-1, keepdims=True))
        alpha = jnp.exp2(m_prev - m_new)
        p = jnp.exp2(s - pltpu.repeat(m_new, tk // LANES, axis=1))
        pv = _dot(p.astype(BF16), ve_ref[pl.ds(pl.multiple_of(j * tk, tk), tk), :])
        acc_ref[h] = acc_ref[h] * pltpu.repeat(alpha, 2, axis=1) + pv
        m_ref[h] = m_new

    for h in range(hpg):
        s_ref[h] = scores(0, h)
    j_last = s0 // tk

    def body(j, carry):
        for h in range(hpg):
            s_next = scores(j + 1, h)
            consume(s_ref[h], j, h)
            s_ref[h] = s_next
        return carry

    lax.fori_loop(0, j_last, body, 0)
    causal = j_last * tk + lax.broadcasted_iota(jnp.int32, (tq, tk), 1) <= t_col
    for h in range(hpg):
        consume(jnp.where(causal, s_ref[h], MASK_BIAS), j_last, h)

    wk = WINDOW + tq
    start = pl.multiple_of(jnp.maximum(s0 - WINDOW, 0), tq)
    kpos = start + lax.broadcasted_iota(jnp.int32, (tq, wk), 1)
    mask_w = (kpos <= t_col) & (kpos > t_col - WINDOW)
    gates = jax.nn.sigmoid(gl_ref[...].astype(F32))
    for h in range(hpg):
        s_w = jnp.where(mask_w, _dot_nt(qa_ref[h, :, 0:dh], kw_ref[pl.ds(start, wk), :]), -jnp.inf)
        p_w = jnp.exp2(s_w - jnp.max(s_w, axis=-1, keepdims=True))
        r_w = 1.0 / jnp.sum(p_w, axis=-1, keepdims=True)
        o_w = _dot(p_w.astype(BF16), vw_ref[pl.ds(start, wk), :]) * r_w
        acc = acc_ref[h]
        o_s = acc[:, 0:dh] * (1.0 / acc[:, dh:dh + 1])
        out = (gates[:, h:h + 1] * oc_ref[h]
               + gates[:, hpg + h:hpg + h + 1] * o_s
               + gates[:, 2 * hpg + h:2 * hpg + h + 1] * o_w)
        o_ref[:, h * dh:(h + 1) * dh] = out.astype(o_ref.dtype)


def _nsa(P, cmp_kv, overlap, lay, *, B, S, tq=512, tk=512):
    G, dh, hpg = N_KV_GROUPS, HEAD_DIM, HEADS_PER_GROUP
    tq = min(tq, S)
    tk = min(tk, S)
    nk = cmp_kv.shape[3]
    n_sel = min(SEL_TOPK, S // SEL_BLOCK)
    assert S % tq == 0 and tk % tq == 0 and S % tk == 0 and S >= WINDOW + tq
    assert S // SEL_BLOCK <= LANES and nk % LANES == 0
    nq = S // tq
    qw = hpg * dh
    kern = functools.partial(_nsa_kernel, tq=tq, tk=tk, n_sel=n_sel)
    return pl.pallas_call(
        kern,
        grid=(B, G, nq),
        in_specs=[
            pl.BlockSpec((tq, qw), lambda b, g, i: (b * nq + i, lay["q"] // qw + g)),
            pl.BlockSpec((tq, LANES), lambda b, g, i: (b * nq + i, lay["gates"] // LANES + g)),
            pl.BlockSpec((None, None, None, nk, dh), lambda b, g, i: (0, b, g, 0, 0)),
            pl.BlockSpec((None, None, None, nk, dh), lambda b, g, i: (1, b, g, 0, 0)),
            pl.BlockSpec((nk, LANES), lambda b, g, i: (0, 0)),
            pl.BlockSpec((S, dh), lambda b, g, i: (b, lay["ks"] // dh + g)),
            pl.BlockSpec((S, dh), lambda b, g, i: (b, lay["vs"] // dh + g)),
            pl.BlockSpec((S, dh), lambda b, g, i: (b, lay["kw"] // dh + g)),
            pl.BlockSpec((S, dh), lambda b, g, i: (b, lay["vw"] // dh + g)),
        ],
        out_specs=pl.BlockSpec((tq, qw), lambda b, g, i: (b * nq + i, g)),
        out_shape=jax.ShapeDtypeStruct((B * S, G * qw), BF16),
        scratch_shapes=[
            pltpu.VMEM((S, 2 * dh), BF16),
            pltpu.VMEM((S, 2 * dh), BF16),
            pltpu.VMEM((hpg, tq, 2 * dh), BF16),
            pltpu.VMEM((hpg, tq, LANES), F32),
            pltpu.VMEM((hpg, tq, 2 * dh), F32),
            pltpu.VMEM((hpg, tq, tk), F32),
            pltpu.VMEM((LANES, tq), F32),
            pltpu.VMEM((LANES, tq), F32),
            pltpu.VMEM((hpg, tq, dh), F32),
        ],
        compiler_params=_params("parallel", "parallel", "arbitrary"),
        name="nsa",
    )(P, P, cmp_kv, cmp_kv, overlap, P, P, P, P)


def _mix_kernel(xp_ref, halo_ref, b_ref, gp_ref, gn_ref, pw_ref, ps_ref, wpu_ref, wnu_ref, o_ref, *, tm, S):
    halo_rows = halo_ref.shape[0]
    gd = POOL_GROUP_DIM
    t0 = (pl.program_id(0) * tm) % S
    x = xp_ref[...].astype(F32)
    halo = halo_ref[...].astype(F32) * (t0 > 0).astype(F32)
    xc = jnp.concatenate([halo, x], axis=0)
    tpos = t0 + lax.broadcasted_iota(jnp.int32, (tm, 1), 0)
    a_up = jnp.zeros((tm, o_ref.shape[1]), F32)
    for gi, win in enumerate(POOL_WINDOWS):
        cols = slice(gi * gd, (gi + 1) * gd)
        s = xc[:, cols]
        d = 1
        while d < win:
            s = s + pltpu.roll(s, d, 0)
            d *= 2
        cnt = jnp.minimum(tpos + 1, win).astype(F32)
        pooled = s[halo_rows:] / cnt - x[:, cols]
        y = _dot(pooled.astype(BF16), pw_ref[gi]) * ps_ref[:, cols]
        a_up = a_up + _dot(y.astype(BF16), wpu_ref[cols, :])
    b_up = _dot(b_ref[...], wnu_ref[...])
    merged = (jax.nn.sigmoid(gp_ref[...].astype(F32)) * a_up
              + jax.nn.sigmoid(gn_ref[...].astype(F32)) * b_up)
    o_ref[...] = merged.astype(o_ref.dtype)


def _mix(P, b_attn, pool_w, pool_scale, w_pool_up, w_nsa_up, lay, *, S, tm=256):
    T = P.shape[0]
    pw = pool_w.shape[0] * pool_w.shape[1]
    D = w_pool_up.shape[1]
    halo = max(POOL_WINDOWS)
    assert S % tm == 0 and tm % halo == 0 and lay["pool"] == 0
    kern = functools.partial(_mix_kernel, tm=tm, S=S)
    return pl.pallas_call(
        kern,
        grid=(T // tm,),
        in_specs=[
            pl.BlockSpec((tm, pw), lambda i: (i, 0)),
            pl.BlockSpec((halo, pw), lambda i: (jnp.maximum(i * (tm // halo) - 1, 0), 0)),
            pl.BlockSpec((tm, b_attn.shape[1]), lambda i: (i, 0)),
            pl.BlockSpec((tm, D), lambda i: (i, lay["g_pool"] // D)),
            pl.BlockSpec((tm, D), lambda i: (i, lay["g_nsa"] // D)),
            pl.BlockSpec(pool_w.shape, lambda i: (0, 0, 0)),
            pl.BlockSpec((1, pw), lambda i: (0, 0)),
            pl.BlockSpec(w_pool_up.shape, lambda i: (0, 0)),
            pl.BlockSpec(w_nsa_up.shape, lambda i: (0, 0)),
        ],
        out_specs=pl.BlockSpec((tm, D), lambda i: (i, 0)),
        out_shape=jax.ShapeDtypeStruct((T, D), BF16),
        compiler_params=_params("parallel"),
        name="mix",
    )(P, P, b_attn, P, P, pool_w, pool_scale.reshape(1, pw), w_pool_up, w_nsa_up)


def _outproj_kernel(h_ref, m_ref, w_ref, o_ref):
    o_ref[...] = h_ref[...] + _dot(m_ref[...], w_ref[...])


def _outproj(h, merged, w_out, *, tm=512):
    T, D = h.shape
    assert T % tm == 0
    return pl.pallas_call(
        _outproj_kernel,
        grid=(T // tm,),
        in_specs=[
            pl.BlockSpec((tm, D), lambda i: (i, 0)),
            pl.BlockSpec((tm, D), lambda i: (i, 0)),
            pl.BlockSpec(w_out.shape, lambda i: (0, 0)),
        ],
        out_specs=pl.BlockSpec((tm, D), lambda i: (i, 0)),
        out_shape=jax.ShapeDtypeStruct((T, D), F32),
        compiler_params=_params("parallel"),
        name="out_proj",
    )(h, merged, w_out)


def _prep_w_in(w_in, pool_width, d_model):
    G, hpg, dh = N_KV_GROUPS, HEADS_PER_GROUP, HEAD_DIM
    qw, kvw = N_HEADS * dh, G * dh
    o_q = pool_width
    o_kv = o_q + qw
    o_gate = o_kv + 6 * kvw
    o_gp = o_gate + N_NSA_BRANCHES * N_HEADS
    o_gn = o_gp + d_model
    assert w_in.shape[1] == o_gn + d_model
    gate_cols = w_in[:, o_gate:o_gp].reshape(-1, N_NSA_BRANCHES, G, hpg)
    gate_cols = jnp.transpose(gate_cols, (0, 2, 1, 3)).reshape(-1, G, N_NSA_BRANCHES * hpg)
    gate_cols = jnp.pad(gate_cols, ((0, 0), (0, 0), (0, LANES - N_NSA_BRANCHES * hpg))).reshape(-1, G * LANES)
    pieces = [
        w_in[:, :o_q],
        w_in[:, o_q:o_kv] * (LOG2E * dh ** -0.5),
        w_in[:, o_kv:o_gate],
        w_in[:, o_gp:o_gn],
        w_in[:, o_gn:],
        gate_cols,
    ]
    w = jnp.concatenate([p.astype(BF16) for p in pieces], axis=1)
    lay = {"pool": 0, "q": o_q, "kc": o_kv, "vc": o_kv + kvw, "ks": o_kv + 2 * kvw, "vs": o_kv + 3 * kvw,
           "kw": o_kv + 4 * kvw, "vw": o_kv + 5 * kvw, "g_pool": o_gate, "g_nsa": o_gate + d_model,
           "gates": o_gate + 2 * d_model}
    return w, lay


def _overlap_matrix(nk, n_cmp, n_blk):
    n = jnp.arange(nk)[:, None]
    m = jnp.arange(LANES)[None, :]
    cs = n * CMP_STRIDE
    ss = m * SEL_BLOCK
    ov = (cs <= ss + SEL_BLOCK - 1) & (cs + CMP_BLOCK - 1 >= ss) & (n < n_cmp) & (m < n_blk)
    return ov.astype(BF16)


def kernel(x, ffn1_norm, ffn1_w_gate, ffn1_w_up, ffn1_w_down, mix_norm, w_in, pool_w, pool_scale,
           cmp_pe_k, cmp_pe_v, cmp_k_w1, cmp_k_w2, cmp_v_w1, cmp_v_w2, w_pool_up, w_nsa_up, w_out,
           ffn2_norm, ffn2_w_gate, ffn2_w_up, ffn2_w_down, final_norm):
    B, S, D = x.shape
    depth = ffn1_norm.shape[0]
    G, dh = N_KV_GROUPS, HEAD_DIM
    T = B * S
    pool_width = pool_w.shape[1] * pool_w.shape[2]
    n_chunk = S // CMP_STRIDE
    n_cmp = n_chunk - CMP_BLOCK // CMP_STRIDE + 1
    n_blk = S // SEL_BLOCK
    cw = CMP_STRIDE * dh

    h = x.reshape(T, D)
    for l in range(depth):
        last = l == depth - 1
        h = _ffn(h, ffn1_norm[l], ffn1_w_gate[l].astype(BF16), ffn1_w_up[l].astype(BF16),
                 ffn1_w_down[l].astype(BF16))

        w_in_l = w_in.reshape(w_in.shape[1:]) if depth == 1 else w_in[l]
        w_p, lay = _prep_w_in(w_in_l, pool_width, D)
        P = _proj(h, mix_norm[l], w_p)

        w1 = jnp.stack([cmp_k_w1[l], cmp_v_w1[l]]).astype(BF16)
        w2 = jnp.stack([cmp_k_w2[l], cmp_v_w2[l]]).astype(BF16)
        pe = jnp.stack([cmp_pe_k[l], cmp_pe_v[l]]).reshape(2, 1, CMP_BLOCK * dh).astype(BF16)
        cmp_kv = _compress(P, w1, w2, pe, lay, B=B, S=S)

        overlap = _overlap_matrix(n_chunk, n_cmp, n_blk)

        b_attn = _nsa(P, cmp_kv, overlap, lay, B=B, S=S)
        merged = _mix(P, b_attn, pool_w[l].astype(BF16), pool_scale[l], w_pool_up[l].astype(BF16),
                      w_nsa_up[l].astype(BF16), lay, S=S)
        h = _outproj(h, merged, w_out[l].astype(BF16))
        h = _ffn(h, ffn2_norm[l], ffn2_w_gate[l].astype(BF16), ffn2_w_up[l].astype(BF16),
                 ffn2_w_down[l].astype(BF16), final_norm if last else None)
    if depth == 0:
        raise ValueError("depth must be positive")
    return h.reshape(B, S, D)
```
